```python
import math
import jax, jax.numpy as jnp
from jax import lax
import numpy as np

D_MODEL = 1024
BATCH = 4
SEQ = 4096
DEPTH = 2

HEAD_DIM = 64
V_HEAD_DIM = 2 * HEAD_DIM
ATTN_WIDTH = D_MODEL // 2
ATTN_HEADS = ATTN_WIDTH // V_HEAD_DIM
QK_WIDTH = ATTN_HEADS * 2 * HEAD_DIM
Q_BLOCK = 128
ROPE_THETA = 10000.0
LRU_WIDTH = D_MODEL // 2
LRU_BLOCKS = 8
LRU_BLOCK = LRU_WIDTH // LRU_BLOCKS
LRU_C = 8.0
CONV_WIDTH = 4
CONV_PAD = (2, 1)
D_FF = ((8 * D_MODEL // 3 + 255) // 256) * 256
N_EXPERTS = 8
TOP_K = 2
D_FF_EXPERT = 7 * D_MODEL // 2
N_DENSE = (DEPTH + 1) // 2
N_MOE = DEPTH // 2
DEEPNORM_ALPHA = (2.0 * DEPTH) ** 0.25
DEEPNORM_BETA = (8.0 * DEPTH) ** -0.25
LN_EPS = 1e-5
IN_SPLITS = [QK_WIDTH, 2 * QK_WIDTH, 2 * QK_WIDTH + ATTN_WIDTH,
             2 * QK_WIDTH + ATTN_WIDTH + LRU_WIDTH,
             2 * QK_WIDTH + ATTN_WIDTH + 2 * LRU_WIDTH,
             2 * QK_WIDTH + ATTN_WIDTH + 2 * LRU_WIDTH + D_MODEL]
IN_COLS = 2 * QK_WIDTH + ATTN_WIDTH + 2 * LRU_WIDTH + 2 * D_MODEL

kernel_name = "hybrid_diffattn_rglru_moe_encoder"


def layernorm(x, g, b):
    xf = x.astype(jnp.float32)
    mu = jnp.mean(xf, axis=-1, keepdims=True)
    var = jnp.mean(jnp.square(xf - mu), axis=-1, keepdims=True)
    y = (xf - mu) * lax.rsqrt(var + LN_EPS)
    return (y * g.astype(jnp.float32) + b.astype(jnp.float32)).astype(x.dtype)


def rmsnorm(x, g):
    xf = x.astype(jnp.float32)
    y = xf * lax.rsqrt(jnp.mean(jnp.square(xf), axis=-1, keepdims=True) + LN_EPS)
    return (y * g.astype(jnp.float32)).astype(x.dtype)


def rotary(t, pos):
    half = HEAD_DIM // 2
    inv = 1.0 / (ROPE_THETA ** (jnp.arange(half, dtype=jnp.float32) * (2.0 / HEAD_DIM)))
    ang = pos.astype(jnp.float32)[:, None] * inv[None, :]
    cos = jnp.cos(ang)[None, :, None, None, :].astype(t.dtype)
    sin = jnp.sin(ang)[None, :, None, None, :].astype(t.dtype)
    t1, t2 = t[..., :half], t[..., half:]
    return jnp.concatenate([t1 * cos - t2 * sin, t2 * cos + t1 * sin], axis=-1)


def diff_attention(q, k, v, lam):
    B, S = q.shape[0], q.shape[1]
    nblk = S // Q_BLOCK
    qb = (q * (HEAD_DIM ** -0.5)).reshape(B, nblk, Q_BLOCK, ATTN_HEADS, 2, HEAD_DIM)
    qb = jnp.moveaxis(qb, 1, 0)

    def one_block(qblk):
        s = jnp.einsum('bqhmd,bkhmd->bhmqk', qblk, k).astype(jnp.float32)
        p = jax.nn.softmax(s, axis=-1)
        w = p[:, :, 0] - lam * p[:, :, 1]
        return jnp.einsum('bhqk,bkhe->bqhe', w.astype(v.dtype), v)

    out = lax.map(one_block, qb)
    return jnp.moveaxis(out, 0, 1).reshape(B, S, ATTN_HEADS, V_HEAD_DIM)


def linear_scan(a, b, reverse):
    def comb(l, r):
        return (l[0] * r[0], r[0] * l[1] + r[1])
    _, h = lax.associative_scan(comb, (a, b), axis=1, reverse=reverse)
    return h


def rg_lru(xc, wa, ba, wx, bx, lam, reverse):
    B, S, C = xc.shape
    xb = xc.reshape(B, S, LRU_BLOCKS, LRU_BLOCK)
    r = jax.nn.sigmoid(jnp.einsum('bsnc,ncd->bsnd', xb, wa).reshape(B, S, C) + ba)
    i = jax.nn.sigmoid(jnp.einsum('bsnc,ncd->bsnd', xb, wx).reshape(B, S, C) + bx)
    log_a = -LRU_C * r.astype(jnp.float32) * jax.nn.softplus(-lam.astype(jnp.float32))
    a = jnp.exp(log_a)
    b = jnp.sqrt(-jnp.expm1(2.0 * log_a)) * (i * xc).astype(jnp.float32)
    return linear_scan(a, b, reverse).astype(xc.dtype)


def depthwise_conv(x, w, b):
    C = x.shape[-1]
    y = lax.conv_general_dilated(x, w[:, None, :].astype(x.dtype), window_strides=(1,),
                                 padding=[CONV_PAD], dimension_numbers=('NWC', 'WIO', 'NWC'),
                                 feature_group_count=C)
    return y + b


def swiglu(h, wg, wu, wd):
    return (jax.nn.silu(h @ wg) * (h @ wu)) @ wd


def moe_swiglu(h, wr, br, wg, wu, wd):
    B, S, D = h.shape
    ht = h.reshape(B * S, D)
    logits = (ht @ wr + br).astype(jnp.float32)
    top_v, top_i = lax.top_k(logits, TOP_K)
    gates = jax.nn.softmax(top_v, axis=-1)
    combine = jnp.sum(jax.nn.one_hot(top_i, N_EXPERTS, dtype=jnp.float32) * gates[..., None], axis=1)
    combine = combine.astype(h.dtype)
    y = jnp.zeros_like(ht)
    for e in range(N_EXPERTS):
        y = y + combine[:, e:e + 1] * swiglu(ht, wg[e], wu[e], wd[e])
    return y.reshape(B, S, D)


def setup_inputs(seed: int = 0) -> dict:
    key = jax.random.key(seed)
    keys = iter(jax.random.split(key, 64))

    def nrm(shape, scale):
        return jax.random.normal(next(keys), shape, jnp.float32) * scale

    def gain(shape):
        return 1.0 + nrm(shape, 0.02)

    L = DEPTH
    C = LRU_WIDTH
    u = jax.random.uniform(next(keys), (L, 2, C), jnp.float32, 0.9, 0.999)
    a0 = u ** (1.0 / LRU_C)
    rg_lam = jnp.log(a0) - jnp.log1p(-a0)
    beta = DEEPNORM_BETA
    return {
        "x": nrm((BATCH, SEQ, D_MODEL), 1.0),
        "ln_in_g": gain((D_MODEL,)),
        "ln_in_b": nrm((D_MODEL,), 0.02),
        "w_in": nrm((L, D_MODEL, IN_COLS), D_MODEL ** -0.5),
        "b_in": nrm((L, IN_COLS), 0.01),
        "lam_q1": nrm((L, HEAD_DIM), 0.1),
        "lam_k1": nrm((L, HEAD_DIM), 0.1),
        "lam_q2": nrm((L, HEAD_DIM), 0.1),
        "lam_k2": nrm((L, HEAD_DIM), 0.1),
        "subln_g": gain((L, V_HEAD_DIM)),
        "conv_w": nrm((L, CONV_WIDTH, C), CONV_WIDTH ** -0.5),
        "conv_b": nrm((L, C), 0.01),
        "rg_wa": nrm((L, 2, LRU_BLOCKS, LRU_BLOCK, LRU_BLOCK), LRU_BLOCK ** -0.5),
        "rg_ba": nrm((L, 2, C), 0.01),
        "rg_wx": nrm((L, 2, LRU_BLOCKS, LRU_BLOCK, LRU_BLOCK), LRU_BLOCK ** -0.5),
        "rg_bx": nrm((L, 2, C), 0.01),
        "rg_lam": rg_lam,
        "w_pa": nrm((L, ATTN_WIDTH, D_MODEL), ATTN_WIDTH ** -0.5 * beta),
        "w_pb": nrm((L, LRU_WIDTH, D_MODEL), LRU_WIDTH ** -0.5 * beta),
        "w_o": nrm((L, D_MODEL, D_MODEL), D_MODEL ** -0.5 * beta),
        "b_o": nrm((L, D_MODEL), 0.01),
        "ln1_g": gain((L, D_MODEL)),
        "ln1_b": nrm((L, D_MODEL), 0.02),
        "ffn_wg": nrm((N_DENSE, D_MODEL, D_FF), D_MODEL ** -0.5),
        "ffn_wu": nrm((N_DENSE, D_MODEL, D_FF), D_MODEL ** -0.5 * beta),
        "ffn_wd": nrm((N_DENSE, D_FF, D_MODEL), D_FF ** -0.5 * beta),
        "moe_wr": nrm((N_MOE, D_MODEL, N_EXPERTS), D_MODEL ** -0.5),
        "moe_br": nrm((N_MOE, N_EXPERTS), 0.01),
        "moe_wg": nrm((N_MOE, N_EXPERTS, D_MODEL, D_FF_EXPERT), D_MODEL ** -0.5),
        "moe_wu": nrm((N_MOE, N_EXPERTS, D_MODEL, D_FF_EXPERT), D_MODEL ** -0.5 * beta),
        "moe_wd": nrm((N_MOE, N_EXPERTS, D_FF_EXPERT, D_MODEL), D_FF_EXPERT ** -0.5 * beta),
        "ln2_g": gain((L, D_MODEL)),
        "ln2_b": nrm((L, D_MODEL), 0.02),
    }


def reference(x, ln_in_g, ln_in_b, w_in, b_in, lam_q1, lam_k1, lam_q2, lam_k2, subln_g,
              conv_w, conv_b, rg_wa, rg_ba, rg_wx, rg_bx, rg_lam, w_pa, w_pb, w_o, b_o,
              ln1_g, ln1_b, ffn_wg, ffn_wu, ffn_wd, moe_wr, moe_br, moe_wg, moe_wu, moe_wd,
              ln2_g, ln2_b):
    B, S, _ = x.shape
    pos = jnp.arange(S, dtype=jnp.int32)
    h = layernorm(x, ln_in_g, ln_in_b)
    for l in range(DEPTH):
        proj = h @ w_in[l] + b_in[l]
        q, k, v, xr, gr, ga, gb = jnp.split(proj, IN_SPLITS, axis=-1)
        q = rotary(q.reshape(B, S, ATTN_HEADS, 2, HEAD_DIM), pos)
        k = rotary(k.reshape(B, S, ATTN_HEADS, 2, HEAD_DIM), pos)
        v = v.reshape(B, S, ATTN_HEADS, V_HEAD_DIM)
        lambda_init = 0.8 - 0.6 * math.exp(-0.3 * l)
        lam = (jnp.exp(jnp.sum(lam_q1[l].astype(jnp.float32) * lam_k1[l].astype(jnp.float32)))
               - jnp.exp(jnp.sum(lam_q2[l].astype(jnp.float32) * lam_k2[l].astype(jnp.float32)))
               + lambda_init)
        att = diff_attention(q, k, v, lam)
        att = rmsnorm(att, subln_g[l]) * (1.0 - lambda_init)
        branch_a = att.reshape(B, S, ATTN_WIDTH) @ w_pa[l]
        xc = depthwise_conv(xr, conv_w[l], conv_b[l])
        h_fwd = rg_lru(xc, rg_wa[l, 0], rg_ba[l, 0], rg_wx[l, 0], rg_bx[l, 0], rg_lam[l, 0], False)
        h_bwd = rg_lru(xc, rg_wa[l, 1], rg_ba[l, 1], rg_wx[l, 1], rg_bx[l, 1], rg_lam[l, 1], True)
        branch_b = ((h_fwd + h_bwd) * jax.nn.gelu(gr)) @ w_pb[l]
        merged = jax.nn.sigmoid(ga) * branch_a + jax.nn.sigmoid(gb) * branch_b
        m = merged @ w_o[l] + b_o[l]
        h = layernorm(DEEPNORM_ALPHA * h + m, ln1_g[l], ln1_b[l])
        if l % 2 == 0:
            f = swiglu(h, ffn_wg[l // 2], ffn_wu[l // 2], ffn_wd[l // 2])
        else:
            j = l // 2
            f = moe_swiglu(h, moe_wr[j], moe_br[j], moe_wg[j], moe_wu[j], moe_wd[j])
        h = layernorm(DEEPNORM_ALPHA * h + f, ln2_g[l], ln2_b[l])
    return h
```

```python
import functools
import math

import jax
import jax.numpy as jnp
from jax import lax
from jax.experimental import pallas as pl
from jax.experimental.pallas import tpu as pltpu

F32 = jnp.float32
BF16 = jnp.bfloat16

HEAD_DIM = 64
V_HEAD_DIM = 2 * HEAD_DIM
LANES = 128
SUBLANES = 8
MXU_DIM = 256
ROPE_THETA = 10000.0
LRU_BLOCK = 64
LRU_C = 8.0
CONV_WIDTH = 4
TOP_K = 2
LN_EPS = 1e-5
VMEM_LIMIT = 52 * 1024 * 1024


def _cparams(n_axes, vmem=VMEM_LIMIT):
    return pltpu.CompilerParams(dimension_semantics=("arbitrary",) * n_axes, vmem_limit_bytes=vmem)


def _layernorm(x, g, b):
    mu = jnp.mean(x, axis=-1, keepdims=True)
    xc = x - mu
    var = jnp.mean(xc * xc, axis=-1, keepdims=True)
    return xc * lax.rsqrt(var + LN_EPS) * g + b


def _gelu_tanh(x):
    return 0.5 * x * (1.0 + jnp.tanh(math.sqrt(2.0 / math.pi) * (x + 0.044715 * (x * x * x))))


def _resident(shape):
    nd = len(shape)
    return pl.BlockSpec(shape, lambda *_: (0,) * nd, pipeline_mode=pl.Buffered(1))


def _inproj_kernel(x_ref, lng_ref, lnb_ref, w_ref, b_ref, cos_ref, sin_ref, *outs,
                   apply_ln, qk_w, v_w, lru_w, d_model):
    if apply_ln:
        h_ref, qt_ref, k_ref, vt_ref, xr_ref, gr_ref, ga_ref, gb_ref = outs
    else:
        qt_ref, k_ref, vt_ref, xr_ref, gr_ref, ga_ref, gb_ref = outs
    x = x_ref[...]
    if apply_ln:
        x = _layernorm(x, lng_ref[...], lnb_ref[...])
        h_ref[...] = x
    hb = x.astype(BF16)
    tm = x.shape[0]

    def proj(lo, width):
        return jnp.dot(hb, w_ref[:, lo:lo + width], preferred_element_type=F32) + b_ref[:, lo:lo + width]

    cos = cos_ref[...]
    sin = sin_ref[...]
    lane = lax.broadcasted_iota(jnp.int32, (tm, LANES), 1)
    first_half = (lane % HEAD_DIM) < (HEAD_DIM // 2)

    def rope(t):
        swapped = jnp.where(first_half,
                            pltpu.roll(t, LANES - HEAD_DIM // 2, axis=1),
                            pltpu.roll(t, HEAD_DIM // 2, axis=1))
        return t * cos + swapped * sin

    off = 0
    q = proj(off, qk_w)
    off += qk_w
    scale = HEAD_DIM ** -0.5
    for c in range(qk_w // LANES):
        r = rope(q[:, c * LANES:(c + 1) * LANES]) * scale
        qt_ref[0, c * LANES:(c + 1) * LANES, :] = r.T.astype(BF16)
    k = proj(off, qk_w)
    off += qk_w
    for c in range(qk_w // LANES):
        k_ref[:, c * LANES:(c + 1) * LANES] = rope(k[:, c * LANES:(c + 1) * LANES]).astype(BF16)
    v = proj(off, v_w)
    off += v_w
    for c in range(v_w // LANES):
        vt_ref[0, c * LANES:(c + 1) * LANES, :] = v[:, c * LANES:(c + 1) * LANES].T.astype(BF16)
    xr_ref[...] = proj(off, lru_w)
    off += lru_w
    gr_ref[...] = _gelu_tanh(proj(off, lru_w)).astype(BF16)
    off += lru_w
    ga_ref[...] = jax.nn.sigmoid(proj(off, d_model)).astype(BF16)
    off += d_model
    gb_ref[...] = jax.nn.sigmoid(proj(off, d_model)).astype(BF16)


def _inproj(x2d, ln_g, ln_b, w, b, cos_t, sin_t, *, apply_ln, batch, seq, qk_w, v_w, lru_w, tm):
    t, d = x2d.shape
    n_cols = w.shape[1]
    spt = seq // tm
    row = lambda i: (i, 0)
    tr = lambda i: (i // spt, 0, i % spt)
    out_shape = [
        jax.ShapeDtypeStruct((batch, qk_w, seq), BF16),
        jax.ShapeDtypeStruct((t, qk_w), BF16),
        jax.ShapeDtypeStruct((batch, v_w, seq), BF16),
        jax.ShapeDtypeStruct((t, lru_w), F32),
        jax.ShapeDtypeStruct((t, lru_w), BF16),
        jax.ShapeDtypeStruct((t, d), BF16),
        jax.ShapeDtypeStruct((t, d), BF16),
    ]
    out_specs = [
        pl.BlockSpec((1, qk_w, tm), tr),
        pl.BlockSpec((tm, qk_w), row),
        pl.BlockSpec((1, v_w, tm), tr),
        pl.BlockSpec((tm, lru_w), row),
        pl.BlockSpec((tm, lru_w), row),
        pl.BlockSpec((tm, d), row),
        pl.BlockSpec((tm, d), row),
    ]
    if apply_ln:
        out_shape = [jax.ShapeDtypeStruct((t, d), F32)] + out_shape
        out_specs = [pl.BlockSpec((tm, d), row)] + out_specs
    kern = functools.partial(_inproj_kernel, apply_ln=apply_ln, qk_w=qk_w, v_w=v_w, lru_w=lru_w, d_model=d)
    return pl.pallas_call(
        kern,
        grid=(t // tm,),
        in_specs=[
            pl.BlockSpec((tm, d), row),
            _resident((1, d)), _resident((1, d)),
            _resident((d, n_cols)), _resident((1, n_cols)),
            pl.BlockSpec((tm, LANES), lambda i: (i % spt, 0)),
            pl.BlockSpec((tm, LANES), lambda i: (i % spt, 0)),
        ],
        out_specs=out_specs,
        out_shape=out_shape,
        compiler_params=_cparams(1),
        name="inproj",
    )(x2d, ln_g, ln_b, w, b, cos_t, sin_t)


def _attn_kernel(lam_ref, qt_ref, k_ref, vt_ref, g_ref, o_ref, vaug_ref, *, out_scale):
    i = pl.program_id(2)
    n_ones = vaug_ref.shape[0] - V_HEAD_DIM

    @pl.when(i == 0)
    def _():
        vaug_ref[0:V_HEAD_DIM, :] = vt_ref[0]
        vaug_ref[V_HEAD_DIM:, :] = jnp.ones((n_ones, vaug_ref.shape[1]), BF16)

    qt = qt_ref[0]
    row = lax.broadcasted_iota(jnp.int32, qt.shape, 0)
    zero = jnp.zeros_like(qt)
    k = k_ref[...]
    vaug = vaug_ref[...]

    def one_map(q_half):
        s = jnp.dot(k, q_half, preferred_element_type=F32)
        m = jnp.max(s, axis=0, keepdims=True)
        e = jnp.exp(s - m).astype(BF16)
        o = jnp.dot(vaug, e, preferred_element_type=F32)
        return o[0:V_HEAD_DIM, :] * (1.0 / o[V_HEAD_DIM:V_HEAD_DIM + 1, :])

    o1 = one_map(jnp.where(row < HEAD_DIM, qt, zero))
    o2 = one_map(jnp.where(row >= HEAD_DIM, qt, zero))
    o = (o1 - lam_ref[0] * o2).T
    ms = jnp.mean(o * o, axis=-1, keepdims=True)
    o_ref[...] = (o * lax.rsqrt(ms + LN_EPS) * g_ref[...] * out_scale).astype(o_ref.dtype)


def _attention(lam, qt, k, vt, subln_g, *, out_scale, tq):
    batch, qk_w, seq = qt.shape
    heads = qk_w // V_HEAD_DIM
    nq = seq // tq
    n_ones = 16
    kern = functools.partial(_attn_kernel, out_scale=out_scale)
    return pl.pallas_call(
        kern,
        grid=(batch, heads, nq),
        in_specs=[
            pl.BlockSpec(memory_space=pltpu.SMEM),
            pl.BlockSpec((1, V_HEAD_DIM, tq), lambda b, h, i: (b, h, i)),
            pl.BlockSpec((seq, V_HEAD_DIM), lambda b, h, i: (b, h)),
            pl.BlockSpec((1, V_HEAD_DIM, seq), lambda b, h, i: (b, h, 0)),
            pl.BlockSpec((1, V_HEAD_DIM), lambda b, h, i: (0, 0)),
        ],
        out_specs=pl.BlockSpec((tq, V_HEAD_DIM), lambda b, h, i: (b * nq + i, h)),
        out_shape=jax.ShapeDtypeStruct((batch * seq, qk_w), BF16),
        scratch_shapes=[pltpu.VMEM((V_HEAD_DIM + n_ones, seq), BF16)],
        compiler_params=_cparams(3),
        name="diff_attn",
    )(lam, qt, k, vt, subln_g)


def _lru_kernel(xr_ref, gr_ref, cw_ref, cb_ref, wbd_ref, ba_ref, bx_ref, lam_ref, o_ref,
                xpad_ref, xc_ref, a_ref, b_ref, h_ref, *, chunk):
    seq, cw = xr_ref.shape
    pad = SUBLANES
    zeros = jnp.zeros((pad, cw), F32)
    xpad_ref[0:pad, :] = zeros
    xpad_ref[pad + seq:, :] = zeros
    xpad_ref[pad:pad + seq, :] = xr_ref[...]
    n_chunks = seq // chunk

    for c in range(n_chunks):
        lo = c * chunk
        acc = cb_ref[...] + cw_ref[0:1, :] * xpad_ref[pad + lo - 2:pad + lo - 2 + chunk, :]
        for tap in range(1, CONV_WIDTH):
            s0 = pad + lo - 2 + tap
            acc = acc + cw_ref[tap:tap + 1, :] * xpad_ref[s0:s0 + chunk, :]
        xc_ref[lo:lo + chunk, :] = acc

    row = lax.broadcasted_iota(jnp.int32, (SUBLANES, cw), 0)
    n_blocks = seq // SUBLANES

    for d in range(2):
        neg_lam = -lam_ref[d:d + 1, :]
        softplus = jnp.maximum(neg_lam, 0.0) + jnp.log1p(jnp.exp(-jnp.abs(neg_lam)))
        for c in range(n_chunks):
            lo = c * chunk
            xc = xc_ref[lo:lo + chunk, :]
            xb = xc.astype(BF16)
            r = jax.nn.sigmoid(jnp.dot(xb, wbd_ref[d, 0, 0], preferred_element_type=F32) + ba_ref[d:d + 1, :])
            g = jax.nn.sigmoid(jnp.dot(xb, wbd_ref[d, 1, 0], preferred_element_type=F32) + bx_ref[d:d + 1, :])
            log_a = -LRU_C * r * softplus
            a_ref[lo:lo + chunk, :] = jnp.exp(log_a)
            th = jnp.tanh(log_a)
            b_ref[lo:lo + chunk, :] = jnp.sqrt(-2.0 * th / (1.0 - th)) * (g * xc)

        reverse = d == 1

        def body(j, carry):
            blk = (n_blocks - 1 - j) if reverse else j
            r0 = pl.multiple_of(blk * SUBLANES, SUBLANES)
            a = a_ref[pl.ds(r0, SUBLANES), :]
            b = b_ref[pl.ds(r0, SUBLANES), :]
            for step in (1, 2, 4):
                if reverse:
                    shift, valid = SUBLANES - step, row < SUBLANES - step
                else:
                    shift, valid = step, row >= step
                a_sh = pltpu.roll(a, shift, axis=0)
                b_sh = pltpu.roll(b, shift, axis=0)
                b = jnp.where(valid, a * b_sh + b, b)
                a = jnp.where(valid, a * a_sh, a)
            h = b + a * carry
            if reverse:
                h_ref[pl.ds(r0, SUBLANES), :] = h_ref[pl.ds(r0, SUBLANES), :] + h
                return h[0:1, :]
            h_ref[pl.ds(r0, SUBLANES), :] = h
            return h[SUBLANES - 1:SUBLANES, :]

        lax.fori_loop(0, n_blocks, body, jnp.zeros((1, cw), F32), unroll=4)

    o_ref[...] = (h_ref[...] * gr_ref[...].astype(F32)).astype(o_ref.dtype)


def _lru(xr, gr_act, conv_w, conv_b, wbd, ba, bx, lam, *, batch, seq, cw):
    t, c = xr.shape
    blk = lambda b, j: (b, j)
    par = lambda b, j: (0, j)
    return pl.pallas_call(
        functools.partial(_lru_kernel, chunk=min(512, seq)),
        grid=(batch, c // cw),
        in_specs=[
            pl.BlockSpec((seq, cw), blk),
            pl.BlockSpec((seq, cw), blk),
            pl.BlockSpec((CONV_WIDTH, cw), par),
            pl.BlockSpec((1, cw), par),
            pl.BlockSpec((2, 2, 1, cw, cw), lambda b, j: (0, 0, j, 0, 0)),
            pl.BlockSpec((2, cw), par),
            pl.BlockSpec((2, cw), par),
            pl.BlockSpec((2, cw), par),
        ],
        out_specs=pl.BlockSpec((seq, cw), blk),
        out_shape=jax.ShapeDtypeStruct((t, c), BF16),
        scratch_shapes=[
            pltpu.VMEM((seq + 2 * SUBLANES, cw), F32),
            pltpu.VMEM((seq, cw), F32),
            pltpu.VMEM((seq, cw), F32),
            pltpu.VMEM((seq, cw), F32),
            pltpu.VMEM((seq, cw), F32),
        ],
        compiler_params=_cparams(2),
        name="conv_rglru",
    )(xr, gr_act, conv_w, conv_b, wbd, ba, bx, lam)


def _merge_kernel(att_ref, hb_ref, ga_ref, gb_ref, h_ref, wpa_ref, wpb_ref, wo_ref, bo_ref,
                  lng_ref, lnb_ref, *rest, alpha, n_experts):
    if n_experts:
        wr_ref, br_ref, o_ref, route_ref = rest
    else:
        (o_ref,) = rest
    branch_a = jnp.dot(att_ref[...], wpa_ref[...], preferred_element_type=F32)
    branch_b = jnp.dot(hb_ref[...], wpb_ref[...], preferred_element_type=F32)
    merged = ga_ref[...].astype(F32) * branch_a + gb_ref[...].astype(F32) * branch_b
    m = jnp.dot(merged.astype(BF16), wo_ref[...], preferred_element_type=F32) + bo_ref[...]
    y = _layernorm(alpha * h_ref[...] + m, lng_ref[...], lnb_ref[...])
    o_ref[...] = y
    if n_experts:
        logits = jnp.dot(y, wr_ref[...], preferred_element_type=F32,
                         precision=lax.Precision.HIGHEST) + br_ref[...]
        lane = lax.broadcasted_iota(jnp.int32, logits.shape, 1)
        neg = jnp.float32(-jnp.inf)
        lg = jnp.where(lane < n_experts, logits, neg)
        v1 = jnp.max(lg, axis=-1, keepdims=True)
        i1 = jnp.min(jnp.where(lg == v1, lane, LANES), axis=-1, keepdims=True)
        lg2 = jnp.where(lane == i1, neg, lg)
        v2 = jnp.max(lg2, axis=-1, keepdims=True)
        i2 = jnp.min(jnp.where(lg2 == v2, lane, LANES), axis=-1, keepdims=True)
        e21 = jnp.exp(v2 - v1)
        g1 = 1.0 / (1.0 + e21)
        g2 = e21 * g1
        route = jnp.where(lane == 0, i1.astype(F32),
                          jnp.where(lane == 1, i2.astype(F32),
                                    jnp.where(lane == 2, g1, jnp.where(lane == 3, g2, 0.0))))
        route_ref[...] = route


def _merge(att, hb, ga, gb, h, wpa, wpb, wo, bo, ln_g, ln_b, router, *, alpha, tm):
    t, d = h.shape
    aw = att.shape[1]
    lw = hb.shape[1]
    row = lambda i: (i, 0)
    in_specs = [
        pl.BlockSpec((tm, aw), row), pl.BlockSpec((tm, lw), row),
        pl.BlockSpec((tm, d), row), pl.BlockSpec((tm, d), row), pl.BlockSpec((tm, d), row),
        _resident((aw, d)), _resident((lw, d)), _resident((d, d)), _resident((1, d)),
        _resident((1, d)), _resident((1, d)),
    ]
    args = [att, hb, ga, gb, h, wpa, wpb, wo, bo, ln_g, ln_b]
    out_shape = [jax.ShapeDtypeStruct((t, d), F32)]
    out_specs = [pl.BlockSpec((tm, d), row)]
    n_experts = 0
    if router is not None:
        wr, br, n_experts = router
        in_specs += [_resident((d, LANES)), _resident((1, LANES))]
        args += [wr, br]
        out_shape.append(jax.ShapeDtypeStruct((t, LANES), F32))
        out_specs.append(pl.BlockSpec((tm, LANES), row))
    res = pl.pallas_call(
        functools.partial(_merge_kernel, alpha=alpha, n_experts=n_experts),
        grid=(t // tm,),
        in_specs=in_specs,
        out_specs=out_specs,
        out_shape=out_shape,
        compiler_params=_cparams(1),
        name="merge_outproj",
    )(*args)
    return res if router is not None else res[0]


def _ffn_kernel(h_ref, wg_ref, wu_ref, wd_ref, lng_ref, lnb_ref, o_ref, *, alpha, f_chunk):
    h = h_ref[...]
    hb = h.astype(BF16)
    d_ff = wg_ref.shape[1]
    f = None
    for lo in range(0, d_ff, f_chunk):
        g = jnp.dot(hb, wg_ref[:, lo:lo + f_chunk], preferred_element_type=F32)
        u = jnp.dot(hb, wu_ref[:, lo:lo + f_chunk], preferred_element_type=F32)
        act = (g * jax.nn.sigmoid(g) * u).astype(BF16)
        part = jnp.dot(act, wd_ref[lo:lo + f_chunk, :], preferred_element_type=F32)
        f = part if f is None else f + part
    o_ref[...] = _layernorm(alpha * h + f, lng_ref[...], lnb_ref[...])


def _ffn(h, wg, wu, wd, ln_g, ln_b, *, alpha, tm, f_chunk):
    t, d = h.shape
    d_ff = wg.shape[1]
    row = lambda i: (i, 0)
    return pl.pallas_call(
        functools.partial(_ffn_kernel, alpha=alpha, f_chunk=f_chunk),
        grid=(t // tm,),
        in_specs=[pl.BlockSpec((tm, d), row), _resident((d, d_ff)), _resident((d, d_ff)),
                  _resident((d_ff, d)), _resident((1, d)), _resident((1, d))],
        out_specs=pl.BlockSpec((tm, d), row),
        out_shape=jax.ShapeDtypeStruct((t, d), F32),
        compiler_params=_cparams(1),
        name="dense_swiglu",
    )(h, wg, wu, wd, ln_g, ln_b)


def _row_gather(src_hbm, dst_ref, idx_ref, base, n_rows, sem):
    def issue(r, c):
        tok = idx_ref[base + r]
        pltpu.make_async_copy(src_hbm.at[pl.ds(tok, 1), :], dst_ref.at[pl.ds(r, 1), :], sem).start()
        return c
    lax.fori_loop(0, n_rows, issue, 0)
    pltpu.make_async_copy(src_hbm.at[pl.ds(0, n_rows), :], dst_ref.at[pl.ds(0, n_rows), :], sem).wait()


def _moe_kernel(tile_expert_ref, n_active_ref, src_ref, h_hbm, gate_ref, wg_ref, wu_ref, wd_ref, o_ref,
                xg_ref, xb_ref, acc_ref, sem):
    i = pl.program_id(0)
    j = pl.program_id(1)
    nj = pl.num_programs(1)
    tm = xg_ref.shape[0]
    active = i < n_active_ref[0]

    @pl.when(jnp.logical_and(active, j == 0))
    def _():
        _row_gather(h_hbm, xg_ref, src_ref, i * tm, tm, sem)
        xb_ref[...] = xg_ref[...].astype(BF16)

    @pl.when(active)
    def _():
        xb = xb_ref[...]
        g = jnp.dot(xb, wg_ref[0], preferred_element_type=F32)
        u = jnp.dot(xb, wu_ref[0], preferred_element_type=F32)
        act = (g * jax.nn.sigmoid(g) * u).astype(BF16)
        part = jnp.dot(act, wd_ref[0], preferred_element_type=F32)

        @pl.when(j == 0)
        def _():
            acc_ref[...] = part

        @pl.when(j > 0)
        def _():
            acc_ref[...] += part

    @pl.when(j == nj - 1)
    def _():
        o_ref[...] = jnp.where(active, acc_ref[...] * gate_ref[...], 0.0)


def _moe_experts(tile_expert, n_active, src_tok, h, gate_sorted, wg, wu, wd, *, tm, tf):
    t, d = h.shape
    n_rows = gate_sorted.shape[0]
    d_ff = wg.shape[2]
    grid_spec = pltpu.PrefetchScalarGridSpec(
        num_scalar_prefetch=3,
        grid=(n_rows // tm, d_ff // tf),
        in_specs=[
            pl.BlockSpec(memory_space=pl.ANY),
            pl.BlockSpec((tm, 1), lambda i, j, te, na, src: (i, 0)),
            pl.BlockSpec((1, d, tf), lambda i, j, te, na, src: (te[i], 0, j)),
            pl.BlockSpec((1, d, tf), lambda i, j, te, na, src: (te[i], 0, j)),
            pl.BlockSpec((1, tf, d), lambda i, j, te, na, src: (te[i], j, 0)),
        ],
        out_specs=pl.BlockSpec((tm, d), lambda i, j, te, na, src: (i, 0)),
        scratch_shapes=[
            pltpu.VMEM((tm, d), F32),
            pltpu.VMEM((tm, d), BF16),
            pltpu.VMEM((tm, d), F32),
            pltpu.SemaphoreType.DMA(()),
        ],
    )
    return pl.pallas_call(
        _moe_kernel,
        grid_spec=grid_spec,
        out_shape=jax.ShapeDtypeStruct((n_rows, d), F32),
        compiler_params=_cparams(2),
        name="moe_experts",
    )(tile_expert, n_active, src_tok, h, gate_sorted, wg, wu, wd)


def _combine_kernel(pos_ref, y_hbm, h_ref, lng_ref, lnb_ref, o_ref, ybuf_ref, sem, *, alpha):
    i = pl.program_id(0)
    tm = h_ref.shape[0]
    _row_gather(y_hbm, ybuf_ref, pos_ref, i * (TOP_K * tm), TOP_K * tm, sem)
    f = ybuf_ref[0:tm, :]
    for k in range(1, TOP_K):
        f = f + ybuf_ref[k * tm:(k + 1) * tm, :]
    o_ref[...] = _layernorm(alpha * h_ref[...] + f, lng_ref[...], lnb_ref[...])


def _moe_combine(pos_tiles, y_sorted, h, ln_g, ln_b, *, alpha, tm):
    t, d = h.shape
    grid_spec = pltpu.PrefetchScalarGridSpec(
        num_scalar_prefetch=1,
        grid=(t // tm,),
        in_specs=[
            pl.BlockSpec(memory_space=pl.ANY),
            pl.BlockSpec((tm, d), lambda i, pos: (i, 0)),
            pl.BlockSpec((1, d), lambda i, pos: (0, 0)),
            pl.BlockSpec((1, d), lambda i, pos: (0, 0)),
        ],
        out_specs=pl.BlockSpec((tm, d), lambda i, pos: (i, 0)),
        scratch_shapes=[pltpu.VMEM((TOP_K * tm, d), F32), pltpu.SemaphoreType.DMA(())],
    )
    return pl.pallas_call(
        functools.partial(_combine_kernel, alpha=alpha),
        grid_spec=grid_spec,
        out_shape=jax.ShapeDtypeStruct((t, d), F32),
        compiler_params=_cparams(1),
        name="moe_combine",
    )(pos_tiles, y_sorted, h, ln_g, ln_b)


def _route_plan(route, n_experts, tm_rows, tm_comb):
    t = route.shape[0]
    experts = route[:, 0:TOP_K].astype(jnp.int32).reshape(-1)
    gates = route[:, TOP_K:2 * TOP_K].reshape(-1)
    onehot = (experts[:, None] == jnp.arange(n_experts, dtype=jnp.int32)[None, :]).astype(jnp.int32)
    csum = jnp.cumsum(onehot, axis=0)
    rank = jnp.sum((csum - onehot) * onehot, axis=1)
    counts = csum[-1]
    padded = ((counts + tm_rows - 1) // tm_rows) * tm_rows
    ends = jnp.cumsum(padded)
    starts = ends - padded
    pos = jnp.sum(onehot * starts[None, :], axis=1) + rank
    n_rows = t * TOP_K + n_experts * tm_rows
    tok = jnp.arange(t * TOP_K, dtype=jnp.int32) // TOP_K
    src_tok = jnp.zeros((n_rows,), jnp.int32).at[pos].set(tok)
    gate_sorted = jnp.zeros((n_rows,), F32).at[pos].set(gates)
    n_tiles = n_rows // tm_rows
    n_active = (ends[-1] // tm_rows).astype(jnp.int32)
    tile_start = jnp.arange(n_tiles, dtype=jnp.int32) * tm_rows
    tile_expert = jnp.sum((tile_start[:, None] >= ends[None, :]).astype(jnp.int32), axis=1)
    last_expert = jnp.sum((ends[-1] - 1 >= ends).astype(jnp.int32))
    tile_expert = jnp.minimum(tile_expert, last_expert).astype(jnp.int32)
    pos_tiles = pos.reshape(t // tm_comb, tm_comb, TOP_K).transpose(0, 2, 1).reshape(-1)
    return tile_expert, n_active.reshape(1), src_tok, gate_sorted.reshape(n_rows, 1), pos_tiles


def _pick(n, pref):
    t = min(n, pref)
    assert n % t == 0, (n, pref)
    return t


def kernel(x, ln_in_g, ln_in_b, w_in, b_in, lam_q1, lam_k1, lam_q2, lam_k2, subln_g, conv_w, conv_b, rg_wa, rg_ba, rg_wx, rg_bx, rg_lam, w_pa, w_pb, w_o, b_o, ln1_g, ln1_b, ffn_wg, ffn_wu, ffn_wd, moe_wr, moe_br, moe_wg, moe_wu, moe_wd, ln2_g, ln2_b):
    batch, seq, d = x.shape
    depth = w_in.shape[0]
    t = batch * seq
    attn_w = w_pa.shape[1]
    lru_w = w_pb.shape[1]
    qk_w = (w_in.shape[2] - attn_w - 2 * lru_w - 2 * d) // 2
    n_experts = moe_wr.shape[-1]
    alpha = (2.0 * depth) ** 0.25

    tm_proj = _pick(seq, 512)
    tq = _pick(seq, 256)
    tm_merge = _pick(t, 512)
    tm_ffn = _pick(t, 512)
    tm_moe = _pick(t, 512)
    tm_comb = _pick(t, 256)
    cw = MXU_DIM

    half = HEAD_DIM // 2
    inv = 1.0 / (ROPE_THETA ** (jnp.arange(half, dtype=F32) * (2.0 / HEAD_DIM)))
    ang = jnp.arange(seq, dtype=jnp.int32).astype(F32)[:, None] * inv[None, :]
    cos_t = jnp.tile(jnp.cos(ang), (1, LANES // half))
    sin_t = jnp.tile(jnp.concatenate([-jnp.sin(ang), jnp.sin(ang)], axis=-1), (1, LANES // HEAD_DIM))

    row2 = lambda v: v.reshape(1, -1)
    h = x.reshape(t, d)
    for l in range(depth):
        lambda_init = 0.8 - 0.6 * math.exp(-0.3 * l)
        lam = (jnp.exp(jnp.sum(lam_q1[l] * lam_k1[l])) - jnp.exp(jnp.sum(lam_q2[l] * lam_k2[l]))
               + lambda_init).reshape(1).astype(F32)
        outs = _inproj(h, row2(ln_in_g), row2(ln_in_b), w_in[l].astype(BF16), row2(b_in[l]), cos_t, sin_t,
                       apply_ln=(l == 0), batch=batch, seq=seq, qk_w=qk_w, v_w=attn_w, lru_w=lru_w, tm=tm_proj)
        if l == 0:
            h, *outs = outs
        qt, kk, vt, xr, gr_act, ga, gb = outs

        att = _attention(lam, qt, kk, vt, row2(subln_g[l]), out_scale=1.0 - lambda_init, tq=tq)

        nb = cw // LRU_BLOCK
        def dense_tiles(w):
            w = w.reshape(2, lru_w // cw, nb, LRU_BLOCK, LRU_BLOCK)
            eye = jnp.eye(nb, dtype=w.dtype)
            return jnp.einsum('dtncm,nk->dtnckm', w, eye).reshape(2, lru_w // cw, cw, cw)
        wbd = jnp.stack([dense_tiles(rg_wa[l]), dense_tiles(rg_wx[l])], axis=1).astype(BF16)
        hb = _lru(xr, gr_act, conv_w[l], row2(conv_b[l]), wbd, rg_ba[l], rg_bx[l], rg_lam[l],
                  batch=batch, seq=seq, cw=cw)

        is_moe = l % 2 == 1
        router = None
        if is_moe:
            j = l // 2
            wr = jnp.zeros((d, LANES), F32).at[:, :n_experts].set(moe_wr[j])
            br = jnp.zeros((1, LANES), F32).at[0, :n_experts].set(moe_br[j])
            router = (wr, br, n_experts)
        res = _merge(att, hb, ga, gb, h, w_pa[l].astype(BF16), w_pb[l].astype(BF16), w_o[l].astype(BF16),
                     row2(b_o[l]), row2(ln1_g[l]), row2(ln1_b[l]), router, alpha=alpha, tm=tm_merge)
        if not is_moe:
            h = res
            j = l // 2
            d_ff = ffn_wg.shape[2]
            f_chunk = d_ff // 2 if (d_ff // 2) % LANES == 0 else d_ff
            h = _ffn(h, ffn_wg[j].astype(BF16), ffn_wu[j].astype(BF16), ffn_wd[j].astype(BF16),
                     row2(ln2_g[l]), row2(ln2_b[l]), alpha=alpha, tm=tm_ffn, f_chunk=f_chunk)
        else:
            h, route = res
            d_ff = moe_wg.shape[3]
            tf = d_ff // 2 if (d_ff // 2) % LANES == 0 else d_ff
            tile_expert, n_active, src_tok, gate_sorted, pos_tiles = _route_plan(route, n_experts, tm_moe, tm_comb)
            y_sorted = _moe_experts(tile_expert, n_active, src_tok, h, gate_sorted,
                                    moe_wg[j].astype(BF16), moe_wu[j].astype(BF16), moe_wd[j].astype(BF16),
                                    tm=tm_moe, tf=tf)
            h = _moe_combine(pos_tiles, y_sorted, h, row2(ln2_g[l]), row2(ln2_b[l]), alpha=alpha, tm=tm_comb)
    return h.reshape(batch, seq, d)
```

```python
import functools
import math

import jax
import jax.numpy as jnp
from jax import lax
from jax.experimental import pallas as pl
from jax.experimental.pallas import tpu as pltpu

F32 = jnp.float32
BF16 = jnp.bfloat16

HEAD_DIM = 64
V_HEAD_DIM = 2 * HEAD_DIM
LANES = 128
SUBLANES = 8
MXU_DIM = 256
ROPE_THETA = 10000.0
LRU_BLOCK = 64
LRU_C = 8.0
CONV_WIDTH = 4
TOP_K = 2
LN_EPS = 1e-5
VMEM_LIMIT = 52 * 1024 * 1024


def _cparams(n_axes, vmem=VMEM_LIMIT):
    return pltpu.CompilerParams(dimension_semantics=("arbitrary",) * n_axes, vmem_limit_bytes=vmem)


def _layernorm(x, g, b):
    mu = jnp.mean(x, axis=-1, keepdims=True)
    xc = x - mu
    var = jnp.mean(xc * xc, axis=-1, keepdims=True)
    return xc * lax.rsqrt(var + LN_EPS) * g + b


def _gelu_tanh(x):
    return 0.5 * x * (1.0 + jnp.tanh(math.sqrt(2.0 / math.pi) * (x + 0.044715 * (x * x * x))))


def _resident(shape):
    nd = len(shape)
    return pl.BlockSpec(shape, lambda *_: (0,) * nd, pipeline_mode=pl.Buffered(1))


def _inproj_kernel(x_ref, lng_ref, lnb_ref, w_ref, b_ref, cos_ref, sin_ref, *outs,
                   apply_ln, qk_w, v_w, lru_w, d_model):
    if apply_ln:
        h_ref, qt_ref, k_ref, vt_ref, xr_ref, gr_ref, ga_ref, gb_ref = outs
    else:
        qt_ref, k_ref, vt_ref, xr_ref, gr_ref, ga_ref, gb_ref = outs
    x = x_ref[...]
    if apply_ln:
        x = _layernorm(x, lng_ref[...], lnb_ref[...])
        h_ref[...] = x
    hb = x.astype(BF16)
    tm = x.shape[0]

    def proj(lo, width):
        return jnp.dot(hb, w_ref[:, lo:lo + width], preferred_element_type=F32) + b_ref[:, lo:lo + width]

    cos = cos_ref[...]
    sin = sin_ref[...]
    lane = lax.broadcasted_iota(jnp.int32, (tm, LANES), 1)
    first_half = (lane % HEAD_DIM) < (HEAD_DIM // 2)

    def rope(t):
        swapped = jnp.where(first_half,
                            pltpu.roll(t, LANES - HEAD_DIM // 2, axis=1),
                            pltpu.roll(t, HEAD_DIM // 2, axis=1))
        return t * cos + swapped * sin

    off = 0
    q = proj(off, qk_w)
    off += qk_w
    scale = HEAD_DIM ** -0.5 * math.log2(math.e)
    for c in range(qk_w // LANES):
        r = rope(q[:, c * LANES:(c + 1) * LANES]) * scale
        qt_ref[0, c * LANES:(c + 1) * LANES, :] = r.T.astype(BF16)
    k = proj(off, qk_w)
    off += qk_w
    for c in range(qk_w // LANES):
        k_ref[:, c * LANES:(c + 1) * LANES] = rope(k[:, c * LANES:(c + 1) * LANES]).astype(BF16)
    v = proj(off, v_w)
    off += v_w
    for c in range(v_w // LANES):
        vt_ref[0, c * LANES:(c + 1) * LANES, :] = v[:, c * LANES:(c + 1) * LANES].T.astype(BF16)
    xr_ref[...] = proj(off, lru_w)
    off += lru_w
    gr_ref[...] = _gelu_tanh(proj(off, lru_w)).astype(BF16)
    off += lru_w
    ga_ref[...] = jax.nn.sigmoid(proj(off, d_model)).astype(BF16)
    off += d_model
    gb_ref[...] = jax.nn.sigmoid(proj(off, d_model)).astype(BF16)


def _inproj(x2d, ln_g, ln_b, w, b, cos_t, sin_t, *, apply_ln, batch, seq, qk_w, v_w, lru_w, tm):
    t, d = x2d.shape
    n_cols = w.shape[1]
    spt = seq // tm
    row = lambda i: (i, 0)
    tr = lambda i: (i // spt, 0, i % spt)
    out_shape = [
        jax.ShapeDtypeStruct((batch, qk_w, seq), BF16),
        jax.ShapeDtypeStruct((t, qk_w), BF16),
        jax.ShapeDtypeStruct((batch, v_w, seq), BF16),
        jax.ShapeDtypeStruct((t, lru_w), F32),
        jax.ShapeDtypeStruct((t, lru_w), BF16),
        jax.ShapeDtypeStruct((t, d), BF16),
        jax.ShapeDtypeStruct((t, d), BF16),
    ]
    out_specs = [
        pl.BlockSpec((1, qk_w, tm), tr),
        pl.BlockSpec((tm, qk_w), row),
        pl.BlockSpec((1, v_w, tm), tr),
        pl.BlockSpec((tm, lru_w), row),
        pl.BlockSpec((tm, lru_w), row),
        pl.BlockSpec((tm, d), row),
        pl.BlockSpec((tm, d), row),
    ]
    if apply_ln:
        out_shape = [jax.ShapeDtypeStruct((t, d), F32)] + out_shape
        out_specs = [pl.BlockSpec((tm, d), row)] + out_specs
    kern = functools.partial(_inproj_kernel, apply_ln=apply_ln, qk_w=qk_w, v_w=v_w, lru_w=lru_w, d_model=d)
    return pl.pallas_call(
        kern,
        grid=(t // tm,),
        in_specs=[
            pl.BlockSpec((tm, d), row),
            _resident((1, d)), _resident((1, d)),
            _resident((d, n_cols)), _resident((1, n_cols)),
            pl.BlockSpec((tm, LANES), lambda i: (i % spt, 0)),
            pl.BlockSpec((tm, LANES), lambda i: (i % spt, 0)),
        ],
        out_specs=out_specs,
        out_shape=out_shape,
        compiler_params=_cparams(1),
        name="inproj",
    )(x2d, ln_g, ln_b, w, b, cos_t, sin_t)


def _attn_kernel(lam_ref, qt_ref, k_ref, vt_ref, g_ref, o_ref, vaug_ref, s1_ref, s2_ref, *, out_scale, chunk, qw, lag):
    i = pl.program_id(2)
    n_ones = vaug_ref.shape[0] - V_HEAD_DIM

    @pl.when(i == 0)
    def _():
        vaug_ref[0:V_HEAD_DIM, :] = vt_ref[0]
        vaug_ref[V_HEAD_DIM:, :] = jnp.ones((n_ones, vaug_ref.shape[1]), BF16)

    seq = k_ref.shape[0]
    tq = qt_ref.shape[2]
    n_chunks = seq // chunk
    row = lax.broadcasted_iota(jnp.int32, (V_HEAD_DIM, qw), 0)
    units = [(qb, mp) for qb in range(tq // qw) for mp in range(2)]
    bufs = (s1_ref, s2_ref)

    def q_unit(u):
        qb, mp = units[u]
        q = qt_ref[0, :, qb * qw:(qb + 1) * qw]
        keep = (row < HEAD_DIM) if mp == 0 else (row >= HEAD_DIM)
        return jnp.where(keep, q, jnp.zeros_like(q))

    shift = n_chunks + lag
    assert shift < 2 * n_chunks
    n_tasks = len(units) * n_chunks
    q_cur, m_run, m_chunk, acc, outs = None, None, {}, None, []
    for t in range(n_tasks + shift):
        if t < n_tasks:
            u, c = divmod(t, n_chunks)
            if c == 0:
                q_cur, m_run = q_unit(u), None
            s = jnp.dot(k_ref[c * chunk:(c + 1) * chunk, :], q_cur, preferred_element_type=F32)
            mc = jnp.max(s, axis=0, keepdims=True)
            m_run = mc if m_run is None else jnp.maximum(m_run, mc)
            m_chunk[u, c] = m_run
            bufs[u % 2][c * chunk:(c + 1) * chunk, :] = jnp.exp2((s - m_run).astype(BF16))
        if t >= shift:
            u, c = divmod(t - shift, n_chunks)
            e = bufs[u % 2][c * chunk:(c + 1) * chunk, :]
            part = jnp.dot(vaug_ref[:, c * chunk:(c + 1) * chunk], e, preferred_element_type=F32)
            part = part * jnp.exp2(m_chunk[u, c] - m_chunk[u, n_chunks - 1])
            acc = part if c == 0 else acc + part
            if c == n_chunks - 1:
                outs.append(acc[0:V_HEAD_DIM, :] * (1.0 / acc[V_HEAD_DIM:V_HEAD_DIM + 1, :]))

    for qb in range(tq // qw):
        o = (outs[2 * qb] - lam_ref[0] * outs[2 * qb + 1]).T
        ms = jnp.mean(o * o, axis=-1, keepdims=True)
        o_ref[qb * qw:(qb + 1) * qw, :] = (o * lax.rsqrt(ms + LN_EPS) * g_ref[...] * out_scale).astype(o_ref.dtype)


def _attention(lam, qt, k, vt, subln_g, *, out_scale, tq):
    batch, qk_w, seq = qt.shape
    heads = qk_w // V_HEAD_DIM
    nq = seq // tq
    n_ones = 16
    chunk = _pick(seq, 256)
    kern = functools.partial(_attn_kernel, out_scale=out_scale, chunk=chunk, qw=MXU_DIM,
                             lag=min(4, seq // chunk - 1))
    return pl.pallas_call(
        kern,
        grid=(batch, heads, nq),
        in_specs=[
            pl.BlockSpec(memory_space=pltpu.SMEM),
            pl.BlockSpec((1, V_HEAD_DIM, tq), lambda b, h, i: (b, h, i)),
            pl.BlockSpec((seq, V_HEAD_DIM), lambda b, h, i: (b, h)),
            pl.BlockSpec((1, V_HEAD_DIM, seq), lambda b, h, i: (b, h, 0)),
            pl.BlockSpec((1, V_HEAD_DIM), lambda b, h, i: (0, 0)),
        ],
        out_specs=pl.BlockSpec((tq, V_HEAD_DIM), lambda b, h, i: (b * nq + i, h)),
        out_shape=jax.ShapeDtypeStruct((batch * seq, qk_w), BF16),
        scratch_shapes=[pltpu.VMEM((V_HEAD_DIM + n_ones, seq), BF16),
                        pltpu.VMEM((seq, MXU_DIM), BF16), pltpu.VMEM((seq, MXU_DIM), BF16)],
        compiler_params=_cparams(3),
        name="diff_attn",
    )(lam, qt, k, vt, subln_g)


def _lru_kernel(xr_ref, gr_ref, cw_ref, cb_ref, wbd_ref, ba_ref, bx_ref, lam_ref, o_ref,
                xpad_ref, xc_ref, a_ref, b_ref, h_ref, *, chunk):
    seq, cw = xr_ref.shape
    pad = SUBLANES
    zeros = jnp.zeros((pad, cw), F32)
    xpad_ref[0:pad, :] = zeros
    xpad_ref[pad + seq:, :] = zeros
    xpad_ref[pad:pad + seq, :] = xr_ref[...]
    n_chunks = seq // chunk

    for c in range(n_chunks):
        lo = c * chunk
        acc = cb_ref[...] + cw_ref[0:1, :] * xpad_ref[pad + lo - 2:pad + lo - 2 + chunk, :]
        for tap in range(1, CONV_WIDTH):
            s0 = pad + lo - 2 + tap
            acc = acc + cw_ref[tap:tap + 1, :] * xpad_ref[s0:s0 + chunk, :]
        xc_ref[lo:lo + chunk, :] = acc

    row = lax.broadcasted_iota(jnp.int32, (SUBLANES, cw), 0)
    n_blocks = seq // SUBLANES

    for d in range(2):
        neg_lam = -lam_ref[d:d + 1, :]
        softplus = jnp.maximum(neg_lam, 0.0) + jnp.log1p(jnp.exp(-jnp.abs(neg_lam)))
        for c in range(n_chunks):
            lo = c * chunk
            xc = xc_ref[lo:lo + chunk, :]
            xb = xc.astype(BF16)
            r = jax.nn.sigmoid(jnp.dot(xb, wbd_ref[d, 0, 0], preferred_element_type=F32) + ba_ref[d:d + 1, :])
            g = jax.nn.sigmoid(jnp.dot(xb, wbd_ref[d, 1, 0], preferred_element_type=F32) + bx_ref[d:d + 1, :])
            log_a = -LRU_C * r * softplus
            a_ref[lo:lo + chunk, :] = jnp.exp(log_a)
            th = jnp.tanh(log_a)
            b_ref[lo:lo + chunk, :] = jnp.sqrt(-2.0 * th / (1.0 - th)) * (g * xc)

        reverse = d == 1

        def body(j, carry):
            blk = (n_blocks - 1 - j) if reverse else j
            r0 = pl.multiple_of(blk * SUBLANES, SUBLANES)
            a = a_ref[pl.ds(r0, SUBLANES), :]
            b = b_ref[pl.ds(r0, SUBLANES), :]
            for step in (1, 2, 4):
                if reverse:
                    shift, valid = SUBLANES - step, row < SUBLANES - step
                else:
                    shift, valid = step, row >= step
                a_sh = pltpu.roll(a, shift, axis=0)
                b_sh = pltpu.roll(b, shift, axis=0)
                b = jnp.where(valid, a * b_sh + b, b)
                a = jnp.where(valid, a * a_sh, a)
            h = b + a * carry
            if reverse:
                h_ref[pl.ds(r0, SUBLANES), :] = h_ref[pl.ds(r0, SUBLANES), :] + h
                return h[0:1, :]
            h_ref[pl.ds(r0, SUBLANES), :] = h
            return h[SUBLANES - 1:SUBLANES, :]

        lax.fori_loop(0, n_blocks, body, jnp.zeros((1, cw), F32), unroll=4)

    o_ref[...] = (h_ref[...] * gr_ref[...].astype(F32)).astype(o_ref.dtype)


def _lru(xr, gr_act, conv_w, conv_b, wbd, ba, bx, lam, *, batch, seq, cw):
    t, c = xr.shape
    blk = lambda b, j: (b, j)
    par = lambda b, j: (0, j)
    return pl.pallas_call(
        functools.partial(_lru_kernel, chunk=min(512, seq)),
        grid=(batch, c // cw),
        in_specs=[
            pl.BlockSpec((seq, cw), blk),
            pl.BlockSpec((seq, cw), blk),
            pl.BlockSpec((CONV_WIDTH, cw), par),
            pl.BlockSpec((1, cw), par),
            pl.BlockSpec((2, 2, 1, cw, cw), lambda b, j: (0, 0, j, 0, 0)),
            pl.BlockSpec((2, cw), par),
            pl.BlockSpec((2, cw), par),
            pl.BlockSpec((2, cw), par),
        ],
        out_specs=pl.BlockSpec((seq, cw), blk),
        out_shape=jax.ShapeDtypeStruct((t, c), BF16),
        scratch_shapes=[
            pltpu.VMEM((seq + 2 * SUBLANES, cw), F32),
            pltpu.VMEM((seq, cw), F32),
            pltpu.VMEM((seq, cw), F32),
            pltpu.VMEM((seq, cw), F32),
            pltpu.VMEM((seq, cw), F32),
        ],
        compiler_params=_cparams(2),
        name="conv_rglru",
    )(xr, gr_act, conv_w, conv_b, wbd, ba, bx, lam)


def _merge_kernel(att_ref, hb_ref, ga_ref, gb_ref, h_ref, wpa_ref, wpb_ref, wo_ref, bo_ref,
                  lng_ref, lnb_ref, *rest, alpha, n_experts):
    if n_experts:
        wrh_ref, wrl_ref, br_ref, o_ref, route_ref, counts_ref, cnt_ref = rest
    else:
        (o_ref,) = rest
    branch_a = jnp.dot(att_ref[...], wpa_ref[...], preferred_element_type=F32)
    branch_b = jnp.dot(hb_ref[...], wpb_ref[...], preferred_element_type=F32)
    merged = ga_ref[...].astype(F32) * branch_a + gb_ref[...].astype(F32) * branch_b
    m = jnp.dot(merged.astype(BF16), wo_ref[...], preferred_element_type=F32) + bo_ref[...]
    y = _layernorm(alpha * h_ref[...] + m, lng_ref[...], lnb_ref[...])
    o_ref[...] = y
    if n_experts:
        y_hi = y.astype(BF16)
        y_lo = (y - y_hi.astype(F32)).astype(BF16)
        logits = (jnp.dot(y_hi, wrh_ref[...], preferred_element_type=F32)
                  + jnp.dot(y_lo, wrh_ref[...], preferred_element_type=F32)
                  + jnp.dot(y_hi, wrl_ref[...], preferred_element_type=F32)) + br_ref[...]
        lane = lax.broadcasted_iota(jnp.int32, logits.shape, 1)
        neg = jnp.float32(-jnp.inf)
        lg = jnp.where(lane < n_experts, logits, neg)
        v1 = jnp.max(lg, axis=-1, keepdims=True)
        i1 = jnp.min(jnp.where(lg == v1, lane, LANES), axis=-1, keepdims=True)
        lg2 = jnp.where(lane == i1, neg, lg)
        v2 = jnp.max(lg2, axis=-1, keepdims=True)
        i2 = jnp.min(jnp.where(lg2 == v2, lane, LANES), axis=-1, keepdims=True)
        e21 = jnp.exp(v2 - v1)
        g1 = 1.0 / (1.0 + e21)
        g2 = e21 * g1
        @pl.when(pl.program_id(0) == 0)
        def _():
            cnt_ref[...] = jnp.zeros_like(cnt_ref)

        tm = y.shape[0]
        oh1 = lane == i1
        oh2 = lane == i2
        assign = jnp.where(jnp.logical_or(oh1, oh2), 1.0, 0.0)
        tri = (lax.broadcasted_iota(jnp.int32, (tm, tm), 1)
               < lax.broadcasted_iota(jnp.int32, (tm, tm), 0)).astype(BF16)
        rank = jnp.dot(tri, assign.astype(BF16), preferred_element_type=F32) + cnt_ref[...]
        r1 = jnp.sum(jnp.where(oh1, rank, 0.0), axis=-1, keepdims=True)
        r2 = jnp.sum(jnp.where(oh2, rank, 0.0), axis=-1, keepdims=True)
        cnt_ref[...] += jnp.sum(assign, axis=0, keepdims=True)
        counts_ref[...] = cnt_ref[...]
        cols = (i1.astype(F32), i2.astype(F32), g1, g2, r1, r2)
        route = jnp.zeros_like(logits)
        for n, col in enumerate(cols):
            route = jnp.where(lane == n, col, route)
        route_ref[...] = route


def _merge(att, hb, ga, gb, h, wpa, wpb, wo, bo, ln_g, ln_b, router, *, alpha, tm):
    t, d = h.shape
    aw = att.shape[1]
    lw = hb.shape[1]
    row = lambda i: (i, 0)
    in_specs = [
        pl.BlockSpec((tm, aw), row), pl.BlockSpec((tm, lw), row),
        pl.BlockSpec((tm, d), row), pl.BlockSpec((tm, d), row), pl.BlockSpec((tm, d), row),
        _resident((aw, d)), _resident((lw, d)), _resident((d, d)), _resident((1, d)),
        _resident((1, d)), _resident((1, d)),
    ]
    args = [att, hb, ga, gb, h, wpa, wpb, wo, bo, ln_g, ln_b]
    out_shape = [jax.ShapeDtypeStruct((t, d), F32)]
    out_specs = [pl.BlockSpec((tm, d), row)]
    n_experts = 0
    scratch = []
    if router is not None:
        wr_hi, wr_lo, br, n_experts = router
        in_specs += [_resident((d, LANES)), _resident((d, LANES)), _resident((1, LANES))]
        args += [wr_hi, wr_lo, br]
        out_shape += [jax.ShapeDtypeStruct((t, LANES), F32), jax.ShapeDtypeStruct((1, LANES), F32)]
        out_specs += [pl.BlockSpec((tm, LANES), row), pl.BlockSpec((1, LANES), lambda i: (0, 0))]
        scratch = [pltpu.VMEM((1, LANES), F32)]
    res = pl.pallas_call(
        functools.partial(_merge_kernel, alpha=alpha, n_experts=n_experts),
        grid=(t // tm,),
        in_specs=in_specs,
        out_specs=out_specs,
        out_shape=out_shape,
        scratch_shapes=scratch,
        compiler_params=_cparams(1),
        name="merge_outproj",
    )(*args)
    return res if router is not None else res[0]


def _ffn_kernel(h_ref, wg_ref, wu_ref, wd_ref, lng_ref, lnb_ref, o_ref, *, alpha, f_chunk):
    h = h_ref[...]
    hb = h.astype(BF16)
    d_ff = wg_ref.shape[1]
    f = None
    for lo in range(0, d_ff, f_chunk):
        g = jnp.dot(hb, wg_ref[:, lo:lo + f_chunk], preferred_element_type=F32)
        u = jnp.dot(hb, wu_ref[:, lo:lo + f_chunk], preferred_element_type=F32)
        act = (g * jax.nn.sigmoid(g) * u).astype(BF16)
        part = jnp.dot(act, wd_ref[lo:lo + f_chunk, :], preferred_element_type=F32)
        f = part if f is None else f + part
    o_ref[...] = _layernorm(alpha * h + f, lng_ref[...], lnb_ref[...])


def _ffn(h, wg, wu, wd, ln_g, ln_b, *, alpha, tm, f_chunk):
    t, d = h.shape
    d_ff = wg.shape[1]
    row = lambda i: (i, 0)
    return pl.pallas_call(
        functools.partial(_ffn_kernel, alpha=alpha, f_chunk=f_chunk),
        grid=(t // tm,),
        in_specs=[pl.BlockSpec((tm, d), row), _resident((d, d_ff)), _resident((d, d_ff)),
                  _resident((d_ff, d)), _resident((1, d)), _resident((1, d))],
        out_specs=pl.BlockSpec((tm, d), row),
        out_shape=jax.ShapeDtypeStruct((t, d), F32),
        compiler_params=_cparams(1),
        name="dense_swiglu",
    )(h, wg, wu, wd, ln_g, ln_b)


def _dispatch_kernel(pos_ref, ztile_ref, h_ref, xs_hbm, buf_ref, zero_ref, sems, zsem, *, n_steps, tm_rows):
    i = pl.program_id(0)
    td = h_ref.shape[0]
    slot = i % 2

    @pl.when(i == 0)
    def _():
        zero_ref[...] = jnp.zeros_like(zero_ref)
        parts = tm_rows // td

        def fill(n, c):
            @pl.when(ztile_ref[n] >= 0)
            def _():
                base = ztile_ref[n] * tm_rows
                for part in range(parts):
                    pltpu.make_async_copy(zero_ref, xs_hbm.at[pl.ds(base + part * td, td), :], zsem).start()
            return c
        lax.fori_loop(0, ztile_ref.shape[0], fill, 0)

        def drain(n, c):
            @pl.when(ztile_ref[n] >= 0)
            def _():
                for part in range(parts):
                    pltpu.make_async_copy(zero_ref, xs_hbm.at[pl.ds(0, td), :], zsem).wait()
            return c
        lax.fori_loop(0, ztile_ref.shape[0], drain, 0)

    def wait_slot(s):
        for _ in range(TOP_K):
            pltpu.make_async_copy(buf_ref.at[s], xs_hbm.at[pl.ds(0, td), :], sems.at[s]).wait()

    @pl.when(i >= 2)
    def _():
        wait_slot(slot)

    buf_ref[slot] = h_ref[...]

    def issue(r, c):
        for k in range(TOP_K):
            p = pos_ref[(i * td + r) * TOP_K + k]
            pltpu.make_async_copy(buf_ref.at[slot, pl.ds(r, 1), :], xs_hbm.at[pl.ds(p, 1), :],
                                  sems.at[slot]).start()
        return c
    lax.fori_loop(0, td, issue, 0)

    @pl.when(i == n_steps - 1)
    def _():
        wait_slot(slot)
        if n_steps > 1:
            wait_slot(1 - slot)


def _moe_dispatch(pos_flat, zero_tiles, h, *, n_rows, td, tm_rows):
    t, d = h.shape
    n_steps = t // td
    grid_spec = pltpu.PrefetchScalarGridSpec(
        num_scalar_prefetch=2,
        grid=(n_steps,),
        in_specs=[pl.BlockSpec((td, d), lambda i, pos, zt: (i, 0))],
        out_specs=pl.BlockSpec(memory_space=pl.ANY),
        scratch_shapes=[pltpu.VMEM((2, td, d), F32), pltpu.VMEM((td, d), F32),
                        pltpu.SemaphoreType.DMA((2,)), pltpu.SemaphoreType.DMA(())],
    )
    return pl.pallas_call(
        functools.partial(_dispatch_kernel, n_steps=n_steps, tm_rows=tm_rows),
        grid_spec=grid_spec,
        out_shape=jax.ShapeDtypeStruct((n_rows, d), F32),
        compiler_params=_cparams(1),
        name="moe_dispatch",
    )(pos_flat, zero_tiles, h)


def _moe_kernel(tile_expert_ref, n_active_ref, xs_ref, wg_ref, wu_ref, wd_ref, o_ref, xb_ref, acc_ref):
    i = pl.program_id(0)
    j = pl.program_id(1)
    nj = pl.num_programs(1)
    active = i < n_active_ref[0]

    @pl.when(jnp.logical_and(active, j == 0))
    def _():
        xb_ref[...] = xs_ref[...].astype(BF16)

    @pl.when(active)
    def _():
        xb = xb_ref[...]
        g = jnp.dot(xb, wg_ref[0], preferred_element_type=F32)
        u = jnp.dot(xb, wu_ref[0], preferred_element_type=F32)
        act = (g * jax.nn.sigmoid(g) * u).astype(BF16)
        part = jnp.dot(act, wd_ref[0], preferred_element_type=F32)

        @pl.when(j == 0)
        def _():
            acc_ref[...] = part

        @pl.when(j > 0)
        def _():
            acc_ref[...] += part

    @pl.when(j == nj - 1)
    def _():
        o_ref[...] = jnp.where(active, acc_ref[...], 0.0)


def _moe_experts(tile_expert, n_active, xs, wg, wu, wd, *, tm, tf):
    n_rows, d = xs.shape
    d_ff = wg.shape[2]
    grid_spec = pltpu.PrefetchScalarGridSpec(
        num_scalar_prefetch=2,
        grid=(n_rows // tm, d_ff // tf),
        in_specs=[
            pl.BlockSpec((tm, d), lambda i, j, te, na: (jnp.minimum(i, na[0] - 1), 0)),
            pl.BlockSpec((1, d, tf), lambda i, j, te, na: (te[i], 0, j)),
            pl.BlockSpec((1, d, tf), lambda i, j, te, na: (te[i], 0, j)),
            pl.BlockSpec((1, tf, d), lambda i, j, te, na: (te[i], j, 0)),
        ],
        out_specs=pl.BlockSpec((tm, d), lambda i, j, te, na: (i, 0)),
        scratch_shapes=[pltpu.VMEM((tm, d), BF16), pltpu.VMEM((tm, d), F32)],
    )
    return pl.pallas_call(
        _moe_kernel,
        grid_spec=grid_spec,
        out_shape=jax.ShapeDtypeStruct((n_rows, d), F32),
        compiler_params=_cparams(2),
        name="moe_experts",
    )(tile_expert, n_active, xs, wg, wu, wd)


def _combine_kernel(pos_ref, y_hbm, route_ref, h_ref, lng_ref, lnb_ref, o_ref, ybuf_ref, sems, *, alpha, n_steps):
    i = pl.program_id(0)
    tm = h_ref.shape[0]
    n_rows = TOP_K * tm
    slot = i % 2

    def issue(step, s):
        def body(r, c):
            p = pos_ref[step * n_rows + r]
            pltpu.make_async_copy(y_hbm.at[pl.ds(p, 1), :], ybuf_ref.at[s, pl.ds(r, 1), :], sems.at[s]).start()
            return c
        lax.fori_loop(0, n_rows, body, 0)

    @pl.when(i == 0)
    def _():
        issue(0, 0)

    @pl.when(i + 1 < n_steps)
    def _():
        issue(i + 1, 1 - slot)

    pltpu.make_async_copy(y_hbm.at[pl.ds(0, n_rows), :], ybuf_ref.at[slot], sems.at[slot]).wait()
    f = None
    for k in range(TOP_K):
        term = route_ref[:, TOP_K + k:TOP_K + k + 1] * ybuf_ref[slot, k * tm:(k + 1) * tm, :]
        f = term if f is None else f + term
    o_ref[...] = _layernorm(alpha * h_ref[...] + f, lng_ref[...], lnb_ref[...])


def _moe_combine(pos_tiles, y_sorted, route, h, ln_g, ln_b, *, alpha, tm):
    t, d = h.shape
    n_steps = t // tm
    grid_spec = pltpu.PrefetchScalarGridSpec(
        num_scalar_prefetch=1,
        grid=(n_steps,),
        in_specs=[
            pl.BlockSpec(memory_space=pl.ANY),
            pl.BlockSpec((tm, LANES), lambda i, pos: (i, 0)),
            pl.BlockSpec((tm, d), lambda i, pos: (i, 0)),
            pl.BlockSpec((1, d), lambda i, pos: (0, 0)),
            pl.BlockSpec((1, d), lambda i, pos: (0, 0)),
        ],
        out_specs=pl.BlockSpec((tm, d), lambda i, pos: (i, 0)),
        scratch_shapes=[pltpu.VMEM((2, TOP_K * tm, d), F32), pltpu.SemaphoreType.DMA((2,))],
    )
    return pl.pallas_call(
        functools.partial(_combine_kernel, alpha=alpha, n_steps=n_steps),
        grid_spec=grid_spec,
        out_shape=jax.ShapeDtypeStruct((t, d), F32),
        compiler_params=_cparams(1),
        name="moe_combine",
    )(pos_tiles, y_sorted, route, h, ln_g, ln_b)


def _route_plan(route, counts, n_experts, tm_rows, tm_comb):
    t = route.shape[0]
    experts = route[:, 0:TOP_K].astype(jnp.int32)
    ranks = route[:, 2 * TOP_K:3 * TOP_K].astype(jnp.int32)
    counts = counts[0, :n_experts].astype(jnp.int32)
    padded = ((counts + tm_rows - 1) // tm_rows) * tm_rows
    ends = jnp.cumsum(padded)
    starts = ends - padded
    onehot = (experts[:, :, None] == jnp.arange(n_experts, dtype=jnp.int32)).astype(jnp.int32)
    pos = jnp.sum(onehot * starts, axis=-1) + ranks
    n_rows = t * TOP_K + n_experts * tm_rows
    n_tiles = n_rows // tm_rows
    n_active = (ends[-1] // tm_rows).astype(jnp.int32)
    tile_start = jnp.arange(n_tiles, dtype=jnp.int32) * tm_rows
    tile_expert = jnp.sum((tile_start[:, None] >= ends[None, :]).astype(jnp.int32), axis=1)
    last_expert = jnp.sum((ends[-1] - 1 >= ends).astype(jnp.int32))
    tile_expert = jnp.minimum(tile_expert, last_expert).astype(jnp.int32)
    last_tile = jnp.where(padded > 0, ends // tm_rows - 1, -1)
    tail_tile = n_active + jnp.arange(n_experts, dtype=jnp.int32)
    tail_tile = jnp.where(tail_tile < n_tiles, tail_tile, -1)
    zero_tiles = jnp.concatenate([last_tile, tail_tile]).astype(jnp.int32)
    pos_tiles = pos.reshape(t // tm_comb, tm_comb, TOP_K).transpose(0, 2, 1).reshape(-1)
    return tile_expert, n_active.reshape(1), zero_tiles, pos.reshape(-1), pos_tiles, n_rows


def _pick(n, pref):
    t = min(n, pref)
    assert n % t == 0, (n, pref)
    return t


def kernel(x, ln_in_g, ln_in_b, w_in, b_in, lam_q1, lam_k1, lam_q2, lam_k2, subln_g, conv_w, conv_b, rg_wa, rg_ba, rg_wx, rg_bx, rg_lam, w_pa, w_pb, w_o, b_o, ln1_g, ln1_b, ffn_wg, ffn_wu, ffn_wd, moe_wr, moe_br, moe_wg, moe_wu, moe_wd, ln2_g, ln2_b):
    batch, seq, d = x.shape
    depth = w_in.shape[0]
    t = batch * seq
    attn_w = w_pa.shape[1]
    lru_w = w_pb.shape[1]
    qk_w = (w_in.shape[2] - attn_w - 2 * lru_w - 2 * d) // 2
    n_experts = moe_wr.shape[-1]
    alpha = (2.0 * depth) ** 0.25

    tm_proj = _pick(seq, 512)
    tq = _pick(seq, 1024)
    tm_merge = _pick(t, 512)
    tm_ffn = _pick(t, 512)
    tm_moe = _pick(t, 512)
    tm_comb = _pick(t, 256)
    cw = MXU_DIM

    half = HEAD_DIM // 2
    inv = 1.0 / (ROPE_THETA ** (jnp.arange(half, dtype=F32) * (2.0 / HEAD_DIM)))
    ang = jnp.arange(seq, dtype=jnp.int32).astype(F32)[:, None] * inv[None, :]
    cos_t = jnp.tile(jnp.cos(ang), (1, LANES // half))
    sin_t = jnp.tile(jnp.concatenate([-jnp.sin(ang), jnp.sin(ang)], axis=-1), (1, LANES // HEAD_DIM))

    row2 = lambda v: v.reshape(1, -1)
    h = x.reshape(t, d)
    for l in range(depth):
        lambda_init = 0.8 - 0.6 * math.exp(-0.3 * l)
        lam = (jnp.exp(jnp.sum(lam_q1[l] * lam_k1[l])) - jnp.exp(jnp.sum(lam_q2[l] * lam_k2[l]))
               + lambda_init).reshape(1).astype(F32)
        outs = _inproj(h, row2(ln_in_g), row2(ln_in_b), w_in[l].astype(BF16), row2(b_in[l]), cos_t, sin_t,
                       apply_ln=(l == 0), batch=batch, seq=seq, qk_w=qk_w, v_w=attn_w, lru_w=lru_w, tm=tm_proj)
        if l == 0:
            h, *outs = outs
        qt, kk, vt, xr, gr_act, ga, gb = outs

        att = _attention(lam, qt, kk, vt, row2(subln_g[l]), out_scale=1.0 - lambda_init, tq=tq)

        nb = cw // LRU_BLOCK
        def dense_tiles(w):
            w = w.reshape(2, lru_w // cw, nb, LRU_BLOCK, LRU_BLOCK)
            eye = jnp.eye(nb, dtype=w.dtype)
            return jnp.einsum('dtncm,nk->dtnckm', w, eye).reshape(2, lru_w // cw, cw, cw)
        wbd = jnp.stack([dense_tiles(rg_wa[l]), dense_tiles(rg_wx[l])], axis=1).astype(BF16)
        hb = _lru(xr, gr_act, conv_w[l], row2(conv_b[l]), wbd, rg_ba[l], rg_bx[l], rg_lam[l],
                  batch=batch, seq=seq, cw=cw)

        is_moe = l % 2 == 1
        router = None
        if is_moe:
            j = l // 2
            wr = jnp.zeros((d, LANES), F32).at[:, :n_experts].set(moe_wr[j])
            br = jnp.zeros((1, LANES), F32).at[0, :n_experts].set(moe_br[j])
            wr_hi = wr.astype(BF16)
            wr_lo = (wr - wr_hi.astype(F32)).astype(BF16)
            router = (wr_hi, wr_lo, br, n_experts)
        res = _merge(att, hb, ga, gb, h, w_pa[l].astype(BF16), w_pb[l].astype(BF16), w_o[l].astype(BF16),
                     row2(b_o[l]), row2(ln1_g[l]), row2(ln1_b[l]), router, alpha=alpha, tm=tm_merge)
        if not is_moe:
            h = res
            j = l // 2
            d_ff = ffn_wg.shape[2]
            f_chunk = d_ff // 2 if (d_ff // 2) % LANES == 0 else d_ff
            h = _ffn(h, ffn_wg[j].astype(BF16), ffn_wu[j].astype(BF16), ffn_wd[j].astype(BF16),
                     row2(ln2_g[l]), row2(ln2_b[l]), alpha=alpha, tm=tm_ffn, f_chunk=f_chunk)
        else:
            h, route, counts = res
            d_ff = moe_wg.shape[3]
            tf = d_ff // 2 if (d_ff // 2) % LANES == 0 else d_ff
            tile_expert, n_active, zero_tiles, pos_flat, pos_tiles, n_rows = _route_plan(
                route, counts, n_experts, tm_moe, tm_comb)
            xs = _moe_dispatch(pos_flat, zero_tiles, h, n_rows=n_rows, td=tm_comb, tm_rows=tm_moe)
            y_sorted = _moe_experts(tile_expert, n_active, xs,
                                    moe_wg[j].astype(BF16), moe_wu[j].astype(BF16), moe_wd[j].astype(BF16),
                                    tm=tm_moe, tf=tf)
            h = _moe_combine(pos_tiles, y_sorted, route, h, row2(ln2_g[l]), row2(ln2_b[l]),
                             alpha=alpha, tm=tm_comb)
    return h.reshape(batch, seq, d)
```

```python
import functools
import math

import jax
import jax.numpy as jnp
from jax import lax
from jax.experimental import pallas as pl
from jax.experimental.pallas import tpu as pltpu

F32 = jnp.float32
BF16 = jnp.bfloat16

HEAD_DIM = 64
V_HEAD_DIM = 2 * HEAD_DIM
LANES = 128
SUBLANES = 8
MXU_DIM = 256
ROPE_THETA = 10000.0
LRU_BLOCK = 64
LRU_C = 8.0
CONV_WIDTH = 4
TOP_K = 2
LN_EPS = 1e-5
VMEM_LIMIT = 52 * 1024 * 1024


def _cparams(n_axes, vmem=VMEM_LIMIT):
    return pltpu.CompilerParams(dimension_semantics=("arbitrary",) * n_axes, vmem_limit_bytes=vmem)


def _layernorm(x, g, b):
    mu = jnp.mean(x, axis=-1, keepdims=True)
    xc = x - mu
    var = jnp.mean(xc * xc, axis=-1, keepdims=True)
    return xc * lax.rsqrt(var + LN_EPS) * g + b


def _gelu_tanh(x):
    return 0.5 * x * (1.0 + jnp.tanh(math.sqrt(2.0 / math.pi) * (x + 0.044715 * (x * x * x))))


def _resident(shape):
    nd = len(shape)
    return pl.BlockSpec(shape, lambda *_: (0,) * nd, pipeline_mode=pl.Buffered(1))


def _inproj_kernel(x_ref, lng_ref, lnb_ref, w_ref, b_ref, cos_ref, sin_ref, *outs,
                   apply_ln, qk_w, v_w, lru_w, d_model):
    if apply_ln:
        h_ref, qt_ref, k_ref, vt_ref, xr_ref, gr_ref, ga_ref, gb_ref = outs
    else:
        qt_ref, k_ref, vt_ref, xr_ref, gr_ref, ga_ref, gb_ref = outs
    x = x_ref[...]
    if apply_ln:
        x = _layernorm(x, lng_ref[...], lnb_ref[...])
        h_ref[...] = x
    hb = x.astype(BF16)
    tm = x.shape[0]

    def proj(lo, width):
        return jnp.dot(hb, w_ref[:, lo:lo + width], preferred_element_type=F32) + b_ref[:, lo:lo + width]

    cos = cos_ref[...]
    sin = sin_ref[...]
    lane = lax.broadcasted_iota(jnp.int32, (tm, LANES), 1)
    first_half = (lane % HEAD_DIM) < (HEAD_DIM // 2)

    def rope(t):
        swapped = jnp.where(first_half,
                            pltpu.roll(t, LANES - HEAD_DIM // 2, axis=1),
                            pltpu.roll(t, HEAD_DIM // 2, axis=1))
        return t * cos + swapped * sin

    off = 0
    q = proj(off, qk_w)
    off += qk_w
    scale = HEAD_DIM ** -0.5 * math.log2(math.e)
    for c in range(qk_w // LANES):
        r = rope(q[:, c * LANES:(c + 1) * LANES]) * scale
        qt_ref[0, c * LANES:(c + 1) * LANES, :] = r.T.astype(BF16)
    k = proj(off, qk_w)
    off += qk_w
    for c in range(qk_w // LANES):
        k_ref[:, c * LANES:(c + 1) * LANES] = rope(k[:, c * LANES:(c + 1) * LANES]).astype(BF16)
    v = proj(off, v_w)
    off += v_w
    for c in range(v_w // LANES):
        vt_ref[0, c * LANES:(c + 1) * LANES, :] = v[:, c * LANES:(c + 1) * LANES].T.astype(BF16)
    xr_ref[...] = proj(off, lru_w)
    off += lru_w
    gr_ref[...] = _gelu_tanh(proj(off, lru_w)).astype(BF16)
    off += lru_w
    ga_ref[...] = jax.nn.sigmoid(proj(off, d_model)).astype(BF16)
    off += d_model
    gb_ref[...] = jax.nn.sigmoid(proj(off, d_model)).astype(BF16)


def _inproj(x2d, ln_g, ln_b, w, b, cos_t, sin_t, *, apply_ln, batch, seq, qk_w, v_w, lru_w, tm):
    t, d = x2d.shape
    n_cols = w.shape[1]
    spt = seq // tm
    row = lambda i: (i, 0)
    tr = lambda i: (i // spt, 0, i % spt)
    out_shape = [
        jax.ShapeDtypeStruct((batch, qk_w, seq), BF16),
        jax.ShapeDtypeStruct((t, qk_w), BF16),
        jax.ShapeDtypeStruct((batch, v_w, seq), BF16),
        jax.ShapeDtypeStruct((t, lru_w), F32),
        jax.ShapeDtypeStruct((t, lru_w), BF16),
        jax.ShapeDtypeStruct((t, d), BF16),
        jax.ShapeDtypeStruct((t, d), BF16),
    ]
    out_specs = [
        pl.BlockSpec((1, qk_w, tm), tr),
        pl.BlockSpec((tm, qk_w), row),
        pl.BlockSpec((1, v_w, tm), tr),
        pl.BlockSpec((tm, lru_w), row),
        pl.BlockSpec((tm, lru_w), row),
        pl.BlockSpec((tm, d), row),
        pl.BlockSpec((tm, d), row),
    ]
    if apply_ln:
        out_shape = [jax.ShapeDtypeStruct((t, d), F32)] + out_shape
        out_specs = [pl.BlockSpec((tm, d), row)] + out_specs
    kern = functools.partial(_inproj_kernel, apply_ln=apply_ln, qk_w=qk_w, v_w=v_w, lru_w=lru_w, d_model=d)
    return pl.pallas_call(
        kern,
        grid=(t // tm,),
        in_specs=[
            pl.BlockSpec((tm, d), row),
            _resident((1, d)), _resident((1, d)),
            _resident((d, n_cols)), _resident((1, n_cols)),
            pl.BlockSpec((tm, LANES), lambda i: (i % spt, 0)),
            pl.BlockSpec((tm, LANES), lambda i: (i % spt, 0)),
        ],
        out_specs=out_specs,
        out_shape=out_shape,
        compiler_params=_cparams(1),
        name="inproj",
    )(x2d, ln_g, ln_b, w, b, cos_t, sin_t)


def _attn_kernel(lam_ref, qt_ref, k_ref, vt_ref, g_ref, o_ref, vaug_ref, s1_ref, s2_ref, *, out_scale, chunk, qw, lag):
    i = pl.program_id(2)
    n_ones = vaug_ref.shape[0] - V_HEAD_DIM

    @pl.when(i == 0)
    def _():
        vaug_ref[0:V_HEAD_DIM, :] = vt_ref[0]
        vaug_ref[V_HEAD_DIM:, :] = jnp.ones((n_ones, vaug_ref.shape[1]), BF16)

    seq = k_ref.shape[0]
    tq = qt_ref.shape[2]
    n_chunks = seq // chunk
    row = lax.broadcasted_iota(jnp.int32, (V_HEAD_DIM, qw), 0)
    units = [(qb, mp) for qb in range(tq // qw) for mp in range(2)]
    bufs = (s1_ref, s2_ref)

    def q_unit(u):
        qb, mp = units[u]
        q = qt_ref[0, :, qb * qw:(qb + 1) * qw]
        keep = (row < HEAD_DIM) if mp == 0 else (row >= HEAD_DIM)
        return jnp.where(keep, q, jnp.zeros_like(q))

    shift = n_chunks + lag
    assert shift < 2 * n_chunks
    n_tasks = len(units) * n_chunks
    q_cur, m_run, m_chunk, acc, outs = None, None, {}, None, []
    for t in range(n_tasks + shift):
        if t < n_tasks:
            u, c = divmod(t, n_chunks)
            if c == 0:
                q_cur, m_run = q_unit(u), None
            s = jnp.dot(k_ref[c * chunk:(c + 1) * chunk, :], q_cur, preferred_element_type=F32)
            mc = jnp.max(s, axis=0, keepdims=True)
            m_run = mc if m_run is None else jnp.maximum(m_run, mc)
            m_chunk[u, c] = m_run
            bufs[u % 2][c * chunk:(c + 1) * chunk, :] = jnp.exp2((s - m_run).astype(BF16))
        if t >= shift:
            u, c = divmod(t - shift, n_chunks)
            e = bufs[u % 2][c * chunk:(c + 1) * chunk, :]
            part = jnp.dot(vaug_ref[:, c * chunk:(c + 1) * chunk], e, preferred_element_type=F32)
            part = part * jnp.exp2(m_chunk[u, c] - m_chunk[u, n_chunks - 1])
            acc = part if c == 0 else acc + part
            if c == n_chunks - 1:
                outs.append(acc[0:V_HEAD_DIM, :] * (1.0 / acc[V_HEAD_DIM:V_HEAD_DIM + 1, :]))

    for qb in range(tq // qw):
        o = (outs[2 * qb] - lam_ref[0] * outs[2 * qb + 1]).T
        ms = jnp.mean(o * o, axis=-1, keepdims=True)
        o_ref[qb * qw:(qb + 1) * qw, :] = (o * lax.rsqrt(ms + LN_EPS) * g_ref[...] * out_scale).astype(o_ref.dtype)


def _attention(lam, qt, k, vt, subln_g, *, out_scale, tq):
    batch, qk_w, seq = qt.shape
    heads = qk_w // V_HEAD_DIM
    nq = seq // tq
    n_ones = 16
    chunk = _pick(seq, 256)
    kern = functools.partial(_attn_kernel, out_scale=out_scale, chunk=chunk, qw=MXU_DIM,
                             lag=min(4, seq // chunk - 1))
    return pl.pallas_call(
        kern,
        grid=(batch, heads, nq),
        in_specs=[
            pl.BlockSpec(memory_space=pltpu.SMEM),
            pl.BlockSpec((1, V_HEAD_DIM, tq), lambda b, h, i: (b, h, i)),
            pl.BlockSpec((seq, V_HEAD_DIM), lambda b, h, i: (b, h)),
            pl.BlockSpec((1, V_HEAD_DIM, seq), lambda b, h, i: (b, h, 0)),
            pl.BlockSpec((1, V_HEAD_DIM), lambda b, h, i: (0, 0)),
        ],
        out_specs=pl.BlockSpec((tq, V_HEAD_DIM), lambda b, h, i: (b * nq + i, h)),
        out_shape=jax.ShapeDtypeStruct((batch * seq, qk_w), BF16),
        scratch_shapes=[pltpu.VMEM((V_HEAD_DIM + n_ones, seq), BF16),
                        pltpu.VMEM((seq, MXU_DIM), BF16), pltpu.VMEM((seq, MXU_DIM), BF16)],
        compiler_params=_cparams(3),
        name="diff_attn",
    )(lam, qt, k, vt, subln_g)


def _lru_kernel(xr_ref, gr_ref, cw_ref, cb_ref, wbd_ref, ba_ref, bx_ref, lam_ref, o_ref,
                xpad_ref, xc_ref, a_ref, b_ref, h_ref, *, chunk):
    seq, cw = xr_ref.shape
    pad = SUBLANES
    zeros = jnp.zeros((pad, cw), F32)
    xpad_ref[0:pad, :] = zeros
    xpad_ref[pad + seq:, :] = zeros
    xpad_ref[pad:pad + seq, :] = xr_ref[...]
    n_chunks = seq // chunk

    for c in range(n_chunks):
        lo = c * chunk
        acc = cb_ref[...] + cw_ref[0:1, :] * xpad_ref[pad + lo - 2:pad + lo - 2 + chunk, :]
        for tap in range(1, CONV_WIDTH):
            s0 = pad + lo - 2 + tap
            acc = acc + cw_ref[tap:tap + 1, :] * xpad_ref[s0:s0 + chunk, :]
        xc_ref[lo:lo + chunk, :] = acc

    row = lax.broadcasted_iota(jnp.int32, (SUBLANES, cw), 0)
    n_blocks = seq // SUBLANES

    for d in range(2):
        neg_lam = -lam_ref[d:d + 1, :]
        softplus = jnp.maximum(neg_lam, 0.0) + jnp.log1p(jnp.exp(-jnp.abs(neg_lam)))
        for c in range(n_chunks):
            lo = c * chunk
            xc = xc_ref[lo:lo + chunk, :]
            xb = xc.astype(BF16)
            r = jax.nn.sigmoid(jnp.dot(xb, wbd_ref[d, 0, 0], preferred_element_type=F32) + ba_ref[d:d + 1, :])
            g = jax.nn.sigmoid(jnp.dot(xb, wbd_ref[d, 1, 0], preferred_element_type=F32) + bx_ref[d:d + 1, :])
            log_a = -LRU_C * r * softplus
            a_ref[lo:lo + chunk, :] = jnp.exp(log_a)
            th = jnp.tanh(log_a)
            b_ref[lo:lo + chunk, :] = jnp.sqrt(-2.0 * th / (1.0 - th)) * (g * xc)

        reverse = d == 1

        def body(j, carry):
            blk = (n_blocks - 1 - j) if reverse else j
            r0 = pl.multiple_of(blk * SUBLANES, SUBLANES)
            a = a_ref[pl.ds(r0, SUBLANES), :]
            b = b_ref[pl.ds(r0, SUBLANES), :]
            for step in (1, 2, 4):
                if reverse:
                    shift, valid = SUBLANES - step, row < SUBLANES - step
                else:
                    shift, valid = step, row >= step
                a_sh = pltpu.roll(a, shift, axis=0)
                b_sh = pltpu.roll(b, shift, axis=0)
                b = jnp.where(valid, a * b_sh + b, b)
                a = jnp.where(valid, a * a_sh, a)
            h = b + a * carry
            if reverse:
                h_ref[pl.ds(r0, SUBLANES), :] = h_ref[pl.ds(r0, SUBLANES), :] + h
                return h[0:1, :]
            h_ref[pl.ds(r0, SUBLANES), :] = h
            return h[SUBLANES - 1:SUBLANES, :]

        lax.fori_loop(0, n_blocks, body, jnp.zeros((1, cw), F32), unroll=4)

    o_ref[...] = (h_ref[...] * gr_ref[...].astype(F32)).astype(o_ref.dtype)


def _lru(xr, gr_act, conv_w, conv_b, wbd, ba, bx, lam, *, batch, seq, cw):
    t, c = xr.shape
    blk = lambda b, j: (b, j)
    par = lambda b, j: (0, j)
    return pl.pallas_call(
        functools.partial(_lru_kernel, chunk=min(512, seq)),
        grid=(batch, c // cw),
        in_specs=[
            pl.BlockSpec((seq, cw), blk),
            pl.BlockSpec((seq, cw), blk),
            pl.BlockSpec((CONV_WIDTH, cw), par),
            pl.BlockSpec((1, cw), par),
            pl.BlockSpec((2, 2, 1, cw, cw), lambda b, j: (0, 0, j, 0, 0)),
            pl.BlockSpec((2, cw), par),
            pl.BlockSpec((2, cw), par),
            pl.BlockSpec((2, cw), par),
        ],
        out_specs=pl.BlockSpec((seq, cw), blk),
        out_shape=jax.ShapeDtypeStruct((t, c), BF16),
        scratch_shapes=[
            pltpu.VMEM((seq + 2 * SUBLANES, cw), F32),
            pltpu.VMEM((seq, cw), F32),
            pltpu.VMEM((seq, cw), F32),
            pltpu.VMEM((seq, cw), F32),
            pltpu.VMEM((seq, cw), F32),
        ],
        compiler_params=_cparams(2),
        name="conv_rglru",
    )(xr, gr_act, conv_w, conv_b, wbd, ba, bx, lam)


def _merge_kernel(att_ref, hb_ref, ga_ref, gb_ref, h_ref, wpa_ref, wpb_ref, wo_ref, bo_ref,
                  lng_ref, lnb_ref, *rest, alpha, n_experts):
    if n_experts:
        wrh_ref, wrl_ref, br_ref, o_ref, route_ref, counts_ref, cnt_ref = rest
    else:
        (o_ref,) = rest
    branch_a = jnp.dot(att_ref[...], wpa_ref[...], preferred_element_type=F32)
    branch_b = jnp.dot(hb_ref[...], wpb_ref[...], preferred_element_type=F32)
    merged = ga_ref[...].astype(F32) * branch_a + gb_ref[...].astype(F32) * branch_b
    m = jnp.dot(merged.astype(BF16), wo_ref[...], preferred_element_type=F32) + bo_ref[...]
    y = _layernorm(alpha * h_ref[...] + m, lng_ref[...], lnb_ref[...])
    o_ref[...] = y
    if n_experts:
        y_hi = y.astype(BF16)
        y_lo = (y - y_hi.astype(F32)).astype(BF16)
        logits = (jnp.dot(y_hi, wrh_ref[...], preferred_element_type=F32)
                  + jnp.dot(y_lo, wrh_ref[...], preferred_element_type=F32)
                  + jnp.dot(y_hi, wrl_ref[...], preferred_element_type=F32)) + br_ref[...]
        lane = lax.broadcasted_iota(jnp.int32, logits.shape, 1)
        neg = jnp.float32(-jnp.inf)
        lg = jnp.where(lane < n_experts, logits, neg)
        v1 = jnp.max(lg, axis=-1, keepdims=True)
        i1 = jnp.min(jnp.where(lg == v1, lane, LANES), axis=-1, keepdims=True)
        lg2 = jnp.where(lane == i1, neg, lg)
        v2 = jnp.max(lg2, axis=-1, keepdims=True)
        i2 = jnp.min(jnp.where(lg2 == v2, lane, LANES), axis=-1, keepdims=True)
        e21 = jnp.exp(v2 - v1)
        g1 = 1.0 / (1.0 + e21)
        g2 = e21 * g1
        @pl.when(pl.program_id(0) == 0)
        def _():
            cnt_ref[...] = jnp.zeros_like(cnt_ref)

        tm = y.shape[0]
        oh1 = lane == i1
        oh2 = lane == i2
        assign = jnp.where(jnp.logical_or(oh1, oh2), 1.0, 0.0)
        tri = (lax.broadcasted_iota(jnp.int32, (tm, tm), 1)
               < lax.broadcasted_iota(jnp.int32, (tm, tm), 0)).astype(BF16)
        rank = jnp.dot(tri, assign.astype(BF16), preferred_element_type=F32) + cnt_ref[...]
        r1 = jnp.sum(jnp.where(oh1, rank, 0.0), axis=-1, keepdims=True)
        r2 = jnp.sum(jnp.where(oh2, rank, 0.0), axis=-1, keepdims=True)
        cnt_ref[...] += jnp.sum(assign, axis=0, keepdims=True)
        counts_ref[...] = cnt_ref[...]
        cols = (i1.astype(F32), i2.astype(F32), g1, g2, r1, r2)
        route = jnp.zeros_like(logits)
        for n, col in enumerate(cols):
            route = jnp.where(lane == n, col, route)
        route_ref[...] = route


def _merge(att, hb, ga, gb, h, wpa, wpb, wo, bo, ln_g, ln_b, router, *, alpha, tm):
    t, d = h.shape
    aw = att.shape[1]
    lw = hb.shape[1]
    row = lambda i: (i, 0)
    in_specs = [
        pl.BlockSpec((tm, aw), row), pl.BlockSpec((tm, lw), row),
        pl.BlockSpec((tm, d), row), pl.BlockSpec((tm, d), row), pl.BlockSpec((tm, d), row),
        _resident((aw, d)), _resident((lw, d)), _resident((d, d)), _resident((1, d)),
        _resident((1, d)), _resident((1, d)),
    ]
    args = [att, hb, ga, gb, h, wpa, wpb, wo, bo, ln_g, ln_b]
    out_shape = [jax.ShapeDtypeStruct((t, d), F32)]
    out_specs = [pl.BlockSpec((tm, d), row)]
    n_experts = 0
    scratch = []
    if router is not None:
        wr_hi, wr_lo, br, n_experts = router
        in_specs += [_resident((d, LANES)), _resident((d, LANES)), _resident((1, LANES))]
        args += [wr_hi, wr_lo, br]
        out_shape += [jax.ShapeDtypeStruct((t, LANES), F32), jax.ShapeDtypeStruct((1, LANES), F32)]
        out_specs += [pl.BlockSpec((tm, LANES), row), pl.BlockSpec((1, LANES), lambda i: (0, 0))]
        scratch = [pltpu.VMEM((1, LANES), F32)]
    res = pl.pallas_call(
        functools.partial(_merge_kernel, alpha=alpha, n_experts=n_experts),
        grid=(t // tm,),
        in_specs=in_specs,
        out_specs=out_specs,
        out_shape=out_shape,
        scratch_shapes=scratch,
        compiler_params=_cparams(1),
        name="merge_outproj",
    )(*args)
    return res if router is not None else res[0]


def _ffn_kernel(h_ref, wg_ref, wu_ref, wd_ref, lng_ref, lnb_ref, o_ref, *, alpha, f_chunk):
    h = h_ref[...]
    hb = h.astype(BF16)
    d_ff = wg_ref.shape[1]
    f = None
    for lo in range(0, d_ff, f_chunk):
        g = jnp.dot(hb, wg_ref[:, lo:lo + f_chunk], preferred_element_type=F32)
        u = jnp.dot(hb, wu_ref[:, lo:lo + f_chunk], preferred_element_type=F32)
        act = (g * jax.nn.sigmoid(g) * u).astype(BF16)
        part = jnp.dot(act, wd_ref[lo:lo + f_chunk, :], preferred_element_type=F32)
        f = part if f is None else f + part
    o_ref[...] = _layernorm(alpha * h + f, lng_ref[...], lnb_ref[...])


def _ffn(h, wg, wu, wd, ln_g, ln_b, *, alpha, tm, f_chunk):
    t, d = h.shape
    d_ff = wg.shape[1]
    row = lambda i: (i, 0)
    return pl.pallas_call(
        functools.partial(_ffn_kernel, alpha=alpha, f_chunk=f_chunk),
        grid=(t // tm,),
        in_specs=[pl.BlockSpec((tm, d), row), _resident((d, d_ff)), _resident((d, d_ff)),
                  _resident((d_ff, d)), _resident((1, d)), _resident((1, d))],
        out_specs=pl.BlockSpec((tm, d), row),
        out_shape=jax.ShapeDtypeStruct((t, d), F32),
        compiler_params=_cparams(1),
        name="dense_swiglu",
    )(h, wg, wu, wd, ln_g, ln_b)


def _dispatch_kernel(pos_ref, ztile_ref, h_ref, xs_hbm, buf_ref, zero_ref, sems, zsem, *, n_steps, tm_rows):
    i = pl.program_id(0)
    td = h_ref.shape[0]
    slot = i % 2

    @pl.when(i == 0)
    def _():
        zero_ref[...] = jnp.zeros_like(zero_ref)
        parts = tm_rows // td

        def fill(n, c):
            @pl.when(ztile_ref[n] >= 0)
            def _():
                base = ztile_ref[n] * tm_rows
                for part in range(parts):
                    pltpu.make_async_copy(zero_ref, xs_hbm.at[pl.ds(base + part * td, td), :], zsem).start()
            return c
        lax.fori_loop(0, ztile_ref.shape[0], fill, 0)

        def drain(n, c):
            @pl.when(ztile_ref[n] >= 0)
            def _():
                for part in range(parts):
                    pltpu.make_async_copy(zero_ref, xs_hbm.at[pl.ds(0, td), :], zsem).wait()
            return c
        lax.fori_loop(0, ztile_ref.shape[0], drain, 0)

    def wait_slot(s):
        for _ in range(TOP_K):
            pltpu.make_async_copy(buf_ref.at[s], xs_hbm.at[pl.ds(0, td), :], sems.at[s]).wait()

    @pl.when(i >= 2)
    def _():
        wait_slot(slot)

    buf_ref[slot] = h_ref[...]

    def issue(r, c):
        for k in range(TOP_K):
            p = pos_ref[(i * td + r) * TOP_K + k]
            pltpu.make_async_copy(buf_ref.at[slot, pl.ds(r, 1), :], xs_hbm.at[pl.ds(p, 1), :],
                                  sems.at[slot]).start()
        return c
    lax.fori_loop(0, td, issue, 0, unroll=8)

    @pl.when(i == n_steps - 1)
    def _():
        wait_slot(slot)
        if n_steps > 1:
            wait_slot(1 - slot)


def _moe_dispatch(pos_flat, zero_tiles, h, *, n_rows, td, tm_rows):
    t, d = h.shape
    n_steps = t // td
    grid_spec = pltpu.PrefetchScalarGridSpec(
        num_scalar_prefetch=2,
        grid=(n_steps,),
        in_specs=[pl.BlockSpec((td, d), lambda i, pos, zt: (i, 0))],
        out_specs=pl.BlockSpec(memory_space=pl.ANY),
        scratch_shapes=[pltpu.VMEM((2, td, d), F32), pltpu.VMEM((td, d), F32),
                        pltpu.SemaphoreType.DMA((2,)), pltpu.SemaphoreType.DMA(())],
    )
    return pl.pallas_call(
        functools.partial(_dispatch_kernel, n_steps=n_steps, tm_rows=tm_rows),
        grid_spec=grid_spec,
        out_shape=jax.ShapeDtypeStruct((n_rows, d), F32),
        compiler_params=_cparams(1),
        name="moe_dispatch",
    )(pos_flat, zero_tiles, h)


def _moe_kernel(tile_expert_ref, n_active_ref, xs_ref, wg_ref, wu_ref, wd_ref, o_ref, xb_ref, acc_ref):
    i = pl.program_id(0)
    j = pl.program_id(1)
    nj = pl.num_programs(1)
    active = i < n_active_ref[0]

    @pl.when(jnp.logical_and(active, j == 0))
    def _():
        xb_ref[...] = xs_ref[...].astype(BF16)

    @pl.when(active)
    def _():
        xb = xb_ref[...]
        g = jnp.dot(xb, wg_ref[0], preferred_element_type=F32)
        u = jnp.dot(xb, wu_ref[0], preferred_element_type=F32)
        act = (g * jax.nn.sigmoid(g) * u).astype(BF16)
        part = jnp.dot(act, wd_ref[0], preferred_element_type=F32)

        @pl.when(j == 0)
        def _():
            acc_ref[...] = part

        @pl.when(j > 0)
        def _():
            acc_ref[...] += part

    @pl.when(j == nj - 1)
    def _():
        o_ref[...] = jnp.where(active, acc_ref[...], 0.0)


def _moe_experts(tile_expert, n_active, xs, wg, wu, wd, *, tm, tf):
    n_rows, d = xs.shape
    d_ff = wg.shape[2]
    grid_spec = pltpu.PrefetchScalarGridSpec(
        num_scalar_prefetch=2,
        grid=(n_rows // tm, d_ff // tf),
        in_specs=[
            pl.BlockSpec((tm, d), lambda i, j, te, na: (jnp.minimum(i, na[0] - 1), 0)),
            pl.BlockSpec((1, d, tf), lambda i, j, te, na: (te[i], 0, j)),
            pl.BlockSpec((1, d, tf), lambda i, j, te, na: (te[i], 0, j)),
            pl.BlockSpec((1, tf, d), lambda i, j, te, na: (te[i], j, 0)),
        ],
        out_specs=pl.BlockSpec((tm, d), lambda i, j, te, na: (i, 0)),
        scratch_shapes=[pltpu.VMEM((tm, d), BF16), pltpu.VMEM((tm, d), F32)],
    )
    return pl.pallas_call(
        _moe_kernel,
        grid_spec=grid_spec,
        out_shape=jax.ShapeDtypeStruct((n_rows, d), F32),
        compiler_params=_cparams(2),
        name="moe_experts",
    )(tile_expert, n_active, xs, wg, wu, wd)


def _combine_kernel(pos_ref, y_hbm, route_ref, h_ref, lng_ref, lnb_ref, o_ref, ybuf_ref, sems, *, alpha, n_steps):
    i = pl.program_id(0)
    tm = h_ref.shape[0]
    n_rows = TOP_K * tm
    slot = i % 2

    def issue(step, s):
        def body(r, c):
            p = pos_ref[step * n_rows + r]
            pltpu.make_async_copy(y_hbm.at[pl.ds(p, 1), :], ybuf_ref.at[s, pl.ds(r, 1), :], sems.at[s]).start()
            return c
        lax.fori_loop(0, n_rows, body, 0, unroll=16)

    @pl.when(i == 0)
    def _():
        issue(0, 0)

    @pl.when(i + 1 < n_steps)
    def _():
        issue(i + 1, 1 - slot)

    pltpu.make_async_copy(y_hbm.at[pl.ds(0, n_rows), :], ybuf_ref.at[slot], sems.at[slot]).wait()
    f = None
    for k in range(TOP_K):
        term = route_ref[:, TOP_K + k:TOP_K + k + 1] * ybuf_ref[slot, k * tm:(k + 1) * tm, :]
        f = term if f is None else f + term
    o_ref[...] = _layernorm(alpha * h_ref[...] + f, lng_ref[...], lnb_ref[...])


def _moe_combine(pos_tiles, y_sorted, route, h, ln_g, ln_b, *, alpha, tm):
    t, d = h.shape
    n_steps = t // tm
    grid_spec = pltpu.PrefetchScalarGridSpec(
        num_scalar_prefetch=1,
        grid=(n_steps,),
        in_specs=[
            pl.BlockSpec(memory_space=pl.ANY),
            pl.BlockSpec((tm, LANES), lambda i, pos: (i, 0)),
            pl.BlockSpec((tm, d), lambda i, pos: (i, 0)),
            pl.BlockSpec((1, d), lambda i, pos: (0, 0)),
            pl.BlockSpec((1, d), lambda i, pos: (0, 0)),
        ],
        out_specs=pl.BlockSpec((tm, d), lambda i, pos: (i, 0)),
        scratch_shapes=[pltpu.VMEM((2, TOP_K * tm, d), F32), pltpu.SemaphoreType.DMA((2,))],
    )
    return pl.pallas_call(
        functools.partial(_combine_kernel, alpha=alpha, n_steps=n_steps),
        grid_spec=grid_spec,
        out_shape=jax.ShapeDtypeStruct((t, d), F32),
        compiler_params=_cparams(1),
        name="moe_combine",
    )(pos_tiles, y_sorted, route, h, ln_g, ln_b)


def _route_plan(route, counts, n_experts, tm_rows, tm_comb):
    t = route.shape[0]
    experts = route[:, 0:TOP_K].astype(jnp.int32)
    ranks = route[:, 2 * TOP_K:3 * TOP_K].astype(jnp.int32)
    counts = counts[0, :n_experts].astype(jnp.int32)
    padded = ((counts + tm_rows - 1) // tm_rows) * tm_rows
    ends = jnp.cumsum(padded)
    starts = ends - padded
    onehot = (experts[:, :, None] == jnp.arange(n_experts, dtype=jnp.int32)).astype(jnp.int32)
    pos = jnp.sum(onehot * starts, axis=-1) + ranks
    n_rows = t * TOP_K + n_experts * tm_rows
    n_tiles = n_rows // tm_rows
    n_active = (ends[-1] // tm_rows).astype(jnp.int32)
    tile_start = jnp.arange(n_tiles, dtype=jnp.int32) * tm_rows
    tile_expert = jnp.sum((tile_start[:, None] >= ends[None, :]).astype(jnp.int32), axis=1)
    last_expert = jnp.sum((ends[-1] - 1 >= ends).astype(jnp.int32))
    tile_expert = jnp.minimum(tile_expert, last_expert).astype(jnp.int32)
    last_tile = jnp.where(padded > 0, ends // tm_rows - 1, -1)
    tail_tile = n_active + jnp.arange(n_experts, dtype=jnp.int32)
    tail_tile = jnp.where(tail_tile < n_tiles, tail_tile, -1)
    zero_tiles = jnp.concatenate([last_tile, tail_tile]).astype(jnp.int32)
    pos_tiles = pos.reshape(t // tm_comb, tm_comb, TOP_K).transpose(0, 2, 1).reshape(-1)
    return tile_expert, n_active.reshape(1), zero_tiles, pos.reshape(-1), pos_tiles, n_rows


def _pick(n, pref):
    t = min(n, pref)
    assert n % t == 0, (n, pref)
    return t


def kernel(x, ln_in_g, ln_in_b, w_in, b_in, lam_q1, lam_k1, lam_q2, lam_k2, subln_g, conv_w, conv_b, rg_wa, rg_ba, rg_wx, rg_bx, rg_lam, w_pa, w_pb, w_o, b_o, ln1_g, ln1_b, ffn_wg, ffn_wu, ffn_wd, moe_wr, moe_br, moe_wg, moe_wu, moe_wd, ln2_g, ln2_b):
    batch, seq, d = x.shape
    depth = w_in.shape[0]
    t = batch * seq
    attn_w = w_pa.shape[1]
    lru_w = w_pb.shape[1]
    qk_w = (w_in.shape[2] - attn_w - 2 * lru_w - 2 * d) // 2
    n_experts = moe_wr.shape[-1]
    alpha = (2.0 * depth) ** 0.25

    tm_proj = _pick(seq, 512)
    tq = _pick(seq, 2048)
    tm_merge = _pick(t, 512)
    tm_ffn = _pick(t, 512)
    tm_moe = _pick(t, 512)
    tm_comb = _pick(t, 256)
    cw = MXU_DIM

    half = HEAD_DIM // 2
    inv = 1.0 / (ROPE_THETA ** (jnp.arange(half, dtype=F32) * (2.0 / HEAD_DIM)))
    ang = jnp.arange(seq, dtype=jnp.int32).astype(F32)[:, None] * inv[None, :]
    cos_t = jnp.tile(jnp.cos(ang), (1, LANES // half))
    sin_t = jnp.tile(jnp.concatenate([-jnp.sin(ang), jnp.sin(ang)], axis=-1), (1, LANES // HEAD_DIM))

    row2 = lambda v: v.reshape(1, -1)
    h = x.reshape(t, d)
    for l in range(depth):
        lambda_init = 0.8 - 0.6 * math.exp(-0.3 * l)
        lam = (jnp.exp(jnp.sum(lam_q1[l] * lam_k1[l])) - jnp.exp(jnp.sum(lam_q2[l] * lam_k2[l]))
               + lambda_init).reshape(1).astype(F32)
        outs = _inproj(h, row2(ln_in_g), row2(ln_in_b), w_in[l].astype(BF16), row2(b_in[l]), cos_t, sin_t,
                       apply_ln=(l == 0), batch=batch, seq=seq, qk_w=qk_w, v_w=attn_w, lru_w=lru_w, tm=tm_proj)
        if l == 0:
            h, *outs = outs
        qt, kk, vt, xr, gr_act, ga, gb = outs

        att = _attention(lam, qt, kk, vt, row2(subln_g[l]), out_scale=1.0 - lambda_init, tq=tq)

        nb = cw // LRU_BLOCK
        def dense_tiles(w):
            w = w.reshape(2, lru_w // cw, nb, LRU_BLOCK, LRU_BLOCK)
            eye = jnp.eye(nb, dtype=w.dtype)
            return jnp.einsum('dtncm,nk->dtnckm', w, eye).reshape(2, lru_w // cw, cw, cw)
        wbd = jnp.stack([dense_tiles(rg_wa[l]), dense_tiles(rg_wx[l])], axis=1).astype(BF16)
        hb = _lru(xr, gr_act, conv_w[l], row2(conv_b[l]), wbd, rg_ba[l], rg_bx[l], rg_lam[l],
                  batch=batch, seq=seq, cw=cw)

        is_moe = l % 2 == 1
        router = None
        if is_moe:
            j = l // 2
            wr = jnp.zeros((d, LANES), F32).at[:, :n_experts].set(moe_wr[j])
            br = jnp.zeros((1, LANES), F32).at[0, :n_experts].set(moe_br[j])
            wr_hi = wr.astype(BF16)
            wr_lo = (wr - wr_hi.astype(F32)).astype(BF16)
            router = (wr_hi, wr_lo, br, n_experts)
        res = _merge(att, hb, ga, gb, h, w_pa[l].astype(BF16), w_pb[l].astype(BF16), w_o[l].astype(BF16),
                     row2(b_o[l]), row2(ln1_g[l]), row2(ln1_b[l]), router, alpha=alpha, tm=tm_merge)
        if not is_moe:
            h = res
            j = l // 2
            d_ff = ffn_wg.shape[2]
            f_chunk = d_ff // 2 if (d_ff // 2) % LANES == 0 else d_ff
            h = _ffn(h, ffn_wg[j].astype(BF16), ffn_wu[j].astype(BF16), ffn_wd[j].astype(BF16),
                     row2(ln2_g[l]), row2(ln2_b[l]), alpha=alpha, tm=tm_ffn, f_chunk=f_chunk)
        else:
            h, route, counts = res
            d_ff = moe_wg.shape[3]
            tf = d_ff // 2 if (d_ff // 2) % LANES == 0 else d_ff
            tile_expert, n_active, zero_tiles, pos_flat, pos_tiles, n_rows = _route_plan(
                route, counts, n_experts, tm_moe, tm_comb)
            xs = _moe_dispatch(pos_flat, zero_tiles, h, n_rows=n_rows, td=tm_comb, tm_rows=tm_moe)
            y_sorted = _moe_experts(tile_expert, n_active, xs,
                                    moe_wg[j].astype(BF16), moe_wu[j].astype(BF16), moe_wd[j].astype(BF16),
                                    tm=tm_moe, tf=tf)
            h = _moe_combine(pos_tiles, y_sorted, route, h, row2(ln2_g[l]), row2(ln2_b[l]),
                             alpha=alpha, tm=tm_comb)
    return h.reshape(batch, seq, d)
```

```python
import functools
import math

import jax
import jax.numpy as jnp
from jax import lax
from jax.experimental import pallas as pl
from jax.experimental.pallas import tpu as pltpu

F32 = jnp.float32
BF16 = jnp.bfloat16

HEAD_DIM = 64
V_HEAD_DIM = 2 * HEAD_DIM
LANES = 128
SUBLANES = 8
MXU_DIM = 256
ROPE_THETA = 10000.0
LRU_BLOCK = 64
LRU_C = 8.0
CONV_WIDTH = 4
TOP_K = 2
LN_EPS = 1e-5
VMEM_LIMIT = 52 * 1024 * 1024


def _cparams(n_axes, vmem=VMEM_LIMIT):
    return pltpu.CompilerParams(dimension_semantics=("arbitrary",) * n_axes, vmem_limit_bytes=vmem)


def _layernorm(x, g, b):
    mu = jnp.mean(x, axis=-1, keepdims=True)
    xc = x - mu
    var = jnp.mean(xc * xc, axis=-1, keepdims=True)
    return xc * lax.rsqrt(var + LN_EPS) * g + b


def _gelu_tanh(x):
    return 0.5 * x * (1.0 + jnp.tanh(math.sqrt(2.0 / math.pi) * (x + 0.044715 * (x * x * x))))


def _resident(shape):
    nd = len(shape)
    return pl.BlockSpec(shape, lambda *_: (0,) * nd, pipeline_mode=pl.Buffered(1))


def _inproj_kernel(x_ref, lng_ref, lnb_ref, w_ref, b_ref, cos_ref, sin_ref, *outs,
                   apply_ln, qk_w, v_w, lru_w, d_model):
    if apply_ln:
        h_ref, qt_ref, k_ref, vt_ref, xr_ref, gr_ref, ga_ref, gb_ref = outs
    else:
        qt_ref, k_ref, vt_ref, xr_ref, gr_ref, ga_ref, gb_ref = outs
    x = x_ref[...]
    if apply_ln:
        x = _layernorm(x, lng_ref[...], lnb_ref[...])
        h_ref[...] = x
    hb = x.astype(BF16)
    tm = x.shape[0]

    def proj(lo, width):
        return jnp.dot(hb, w_ref[:, lo:lo + width], preferred_element_type=F32) + b_ref[:, lo:lo + width]

    cos = cos_ref[...]
    sin = sin_ref[...]
    lane = lax.broadcasted_iota(jnp.int32, (tm, LANES), 1)
    first_half = (lane % HEAD_DIM) < (HEAD_DIM // 2)

    def rope(t):
        swapped = jnp.where(first_half,
                            pltpu.roll(t, LANES - HEAD_DIM // 2, axis=1),
                            pltpu.roll(t, HEAD_DIM // 2, axis=1))
        return t * cos + swapped * sin

    off = 0
    q = proj(off, qk_w)
    off += qk_w
    scale = HEAD_DIM ** -0.5 * math.log2(math.e)
    for c in range(qk_w // LANES):
        r = rope(q[:, c * LANES:(c + 1) * LANES]) * scale
        qt_ref[0, c * LANES:(c + 1) * LANES, :] = r.T.astype(BF16)
    k = proj(off, qk_w)
    off += qk_w
    for c in range(qk_w // LANES):
        k_ref[:, c * LANES:(c + 1) * LANES] = rope(k[:, c * LANES:(c + 1) * LANES]).astype(BF16)
    v = proj(off, v_w)
    off += v_w
    for c in range(v_w // LANES):
        vt_ref[0, c * LANES:(c + 1) * LANES, :] = v[:, c * LANES:(c + 1) * LANES].T.astype(BF16)
    xr_ref[...] = proj(off, lru_w)
    off += lru_w
    gr_ref[...] = _gelu_tanh(proj(off, lru_w)).astype(BF16)
    off += lru_w
    ga_ref[...] = jax.nn.sigmoid(proj(off, d_model)).astype(BF16)
    off += d_model
    gb_ref[...] = jax.nn.sigmoid(proj(off, d_model)).astype(BF16)


def _inproj(x2d, ln_g, ln_b, w, b, cos_t, sin_t, *, apply_ln, batch, seq, qk_w, v_w, lru_w, tm):
    t, d = x2d.shape
    n_cols = w.shape[1]
    spt = seq // tm
    row = lambda i: (i, 0)
    tr = lambda i: (i // spt, 0, i % spt)
    out_shape = [
        jax.ShapeDtypeStruct((batch, qk_w, seq), BF16),
        jax.ShapeDtypeStruct((t, qk_w), BF16),
        jax.ShapeDtypeStruct((batch, v_w, seq), BF16),
        jax.ShapeDtypeStruct((t, lru_w), F32),
        jax.ShapeDtypeStruct((t, lru_w), BF16),
        jax.ShapeDtypeStruct((t, d), BF16),
        jax.ShapeDtypeStruct((t, d), BF16),
    ]
    out_specs = [
        pl.BlockSpec((1, qk_w, tm), tr),
        pl.BlockSpec((tm, qk_w), row),
        pl.BlockSpec((1, v_w, tm), tr),
        pl.BlockSpec((tm, lru_w), row),
        pl.BlockSpec((tm, lru_w), row),
        pl.BlockSpec((tm, d), row),
        pl.BlockSpec((tm, d), row),
    ]
    if apply_ln:
        out_shape = [jax.ShapeDtypeStruct((t, d), F32)] + out_shape
        out_specs = [pl.BlockSpec((tm, d), row)] + out_specs
    kern = functools.partial(_inproj_kernel, apply_ln=apply_ln, qk_w=qk_w, v_w=v_w, lru_w=lru_w, d_model=d)
    return pl.pallas_call(
        kern,
        grid=(t // tm,),
        in_specs=[
            pl.BlockSpec((tm, d), row),
            _resident((1, d)), _resident((1, d)),
            _resident((d, n_cols)), _resident((1, n_cols)),
            pl.BlockSpec((tm, LANES), lambda i: (i % spt, 0)),
            pl.BlockSpec((tm, LANES), lambda i: (i % spt, 0)),
        ],
        out_specs=out_specs,
        out_shape=out_shape,
        compiler_params=_cparams(1),
        name="inproj",
    )(x2d, ln_g, ln_b, w, b, cos_t, sin_t)


def _attn_kernel(lam_ref, qt_ref, k_ref, vt_ref, g_ref, *rest, out_scale, chunk, qw, lag, n_cast):
    cast_in = rest[:n_cast]
    o_ref = rest[n_cast]
    cast_out = rest[n_cast + 1:2 * n_cast + 1]
    vaug_ref, s1_ref, s2_ref = rest[2 * n_cast + 1:]
    i = pl.program_id(2)
    n_ones = vaug_ref.shape[0] - V_HEAD_DIM

    for src, dst in zip(cast_in, cast_out):
        dst[...] = src[...].astype(dst.dtype)

    @pl.when(i == 0)
    def _():
        vaug_ref[0:V_HEAD_DIM, :] = vt_ref[0]
        vaug_ref[V_HEAD_DIM:, :] = jnp.ones((n_ones, vaug_ref.shape[1]), BF16)

    seq = k_ref.shape[0]
    tq = qt_ref.shape[2]
    n_chunks = seq // chunk
    row = lax.broadcasted_iota(jnp.int32, (V_HEAD_DIM, qw), 0)
    units = [(qb, mp) for qb in range(tq // qw) for mp in range(2)]
    bufs = (s1_ref, s2_ref)

    def q_unit(u):
        qb, mp = units[u]
        q = qt_ref[0, :, qb * qw:(qb + 1) * qw]
        keep = (row < HEAD_DIM) if mp == 0 else (row >= HEAD_DIM)
        return jnp.where(keep, q, jnp.zeros_like(q))

    shift = n_chunks + lag
    assert shift < 2 * n_chunks
    n_tasks = len(units) * n_chunks
    q_cur, m_run, m_chunk, acc, outs = None, None, {}, None, []
    for t in range(n_tasks + shift):
        if t < n_tasks:
            u, c = divmod(t, n_chunks)
            if c == 0:
                q_cur, m_run = q_unit(u), None
            s = jnp.dot(k_ref[c * chunk:(c + 1) * chunk, :], q_cur, preferred_element_type=F32)
            mc = jnp.max(s, axis=0, keepdims=True)
            m_run = mc if m_run is None else jnp.maximum(m_run, mc)
            m_chunk[u, c] = m_run
            bufs[u % 2][c * chunk:(c + 1) * chunk, :] = jnp.exp2((s - m_run).astype(BF16))
        if t >= shift:
            u, c = divmod(t - shift, n_chunks)
            e = bufs[u % 2][c * chunk:(c + 1) * chunk, :]
            part = jnp.dot(vaug_ref[:, c * chunk:(c + 1) * chunk], e, preferred_element_type=F32)
            part = part * jnp.exp2(m_chunk[u, c] - m_chunk[u, n_chunks - 1])
            acc = part if c == 0 else acc + part
            if c == n_chunks - 1:
                outs.append(acc[0:V_HEAD_DIM, :] * (1.0 / acc[V_HEAD_DIM:V_HEAD_DIM + 1, :]))

    for qb in range(tq // qw):
        o = (outs[2 * qb] - lam_ref[0] * outs[2 * qb + 1]).T
        ms = jnp.mean(o * o, axis=-1, keepdims=True)
        o_ref[qb * qw:(qb + 1) * qw, :] = (o * lax.rsqrt(ms + LN_EPS) * g_ref[...] * out_scale).astype(o_ref.dtype)


def _attention(lam, qt, k, vt, subln_g, *, out_scale, tq, to_bf16=()):
    batch, qk_w, seq = qt.shape
    heads = qk_w // V_HEAD_DIM
    nq = seq // tq
    n_steps = batch * heads * nq
    n_ones = 16
    chunk = _pick(seq, 256)
    kern = functools.partial(_attn_kernel, out_scale=out_scale, chunk=chunk, qw=MXU_DIM,
                             lag=min(4, seq // chunk - 1), n_cast=len(to_bf16))
    cast_specs = []
    for w in to_bf16:
        n_e, n_r, n_c = w.shape
        per = n_e * n_r // n_steps
        assert per * n_steps == n_e * n_r and n_r % per == 0 and per % 16 == 0, (w.shape, n_steps)
        per_e = n_r // per

        def index(b, h, i, per_e=per_e):
            s = (b * heads + h) * nq + i
            return (s // per_e, s % per_e, 0)
        cast_specs.append(pl.BlockSpec((1, per, n_c), index))
    res = pl.pallas_call(
        kern,
        grid=(batch, heads, nq),
        in_specs=[
            pl.BlockSpec(memory_space=pltpu.SMEM),
            pl.BlockSpec((1, V_HEAD_DIM, tq), lambda b, h, i: (b, h, i)),
            pl.BlockSpec((seq, V_HEAD_DIM), lambda b, h, i: (b, h)),
            pl.BlockSpec((1, V_HEAD_DIM, seq), lambda b, h, i: (b, h, 0)),
            pl.BlockSpec((1, V_HEAD_DIM), lambda b, h, i: (0, 0)),
        ] + cast_specs,
        out_specs=[pl.BlockSpec((tq, V_HEAD_DIM), lambda b, h, i: (b * nq + i, h))] + cast_specs,
        out_shape=[jax.ShapeDtypeStruct((batch * seq, qk_w), BF16)]
                  + [jax.ShapeDtypeStruct(w.shape, BF16) for w in to_bf16],
        scratch_shapes=[pltpu.VMEM((V_HEAD_DIM + n_ones, seq), BF16),
                        pltpu.VMEM((seq, MXU_DIM), BF16), pltpu.VMEM((seq, MXU_DIM), BF16)],
        compiler_params=_cparams(3),
        name="diff_attn",
    )(lam, qt, k, vt, subln_g, *to_bf16)
    return res[0], tuple(res[1:])


def _lru_kernel(xr_ref, gr_ref, cw_ref, cb_ref, wbd_ref, ba_ref, bx_ref, lam_ref, o_ref,
                xpad_ref, xc_ref, a_ref, b_ref, h_ref, *, chunk):
    seq, cw = xr_ref.shape
    pad = SUBLANES
    zeros = jnp.zeros((pad, cw), F32)
    xpad_ref[0:pad, :] = zeros
    xpad_ref[pad + seq:, :] = zeros
    xpad_ref[pad:pad + seq, :] = xr_ref[...]
    n_chunks = seq // chunk

    for c in range(n_chunks):
        lo = c * chunk
        acc = cb_ref[...] + cw_ref[0:1, :] * xpad_ref[pad + lo - 2:pad + lo - 2 + chunk, :]
        for tap in range(1, CONV_WIDTH):
            s0 = pad + lo - 2 + tap
            acc = acc + cw_ref[tap:tap + 1, :] * xpad_ref[s0:s0 + chunk, :]
        xc_ref[lo:lo + chunk, :] = acc

    row = lax.broadcasted_iota(jnp.int32, (SUBLANES, cw), 0)
    n_blocks = seq // SUBLANES

    for d in range(2):
        neg_lam = -lam_ref[d:d + 1, :]
        softplus = jnp.maximum(neg_lam, 0.0) + jnp.log1p(jnp.exp(-jnp.abs(neg_lam)))
        for c in range(n_chunks):
            lo = c * chunk
            xc = xc_ref[lo:lo + chunk, :]
            xb = xc.astype(BF16)
            zr = jnp.dot(xb, wbd_ref[d, 0, 0], preferred_element_type=F32) + ba_ref[d:d + 1, :]
            zg = jnp.dot(xb, wbd_ref[d, 1, 0], preferred_element_type=F32) + bx_ref[d:d + 1, :]
            r = 0.5 + 0.5 * jnp.tanh(0.5 * zr)
            g = 0.5 + 0.5 * jnp.tanh(0.5 * zg)
            log_a = -LRU_C * r * softplus
            a_ref[lo:lo + chunk, :] = jnp.exp(log_a)
            th = jnp.tanh(log_a)
            u = -2.0 * th
            root = jnp.where(u > 0.0, u * lax.rsqrt(u * (1.0 - th)), 0.0)
            b_ref[lo:lo + chunk, :] = root * (g * xc)

        reverse = d == 1

        def body(j, carry):
            blk = (n_blocks - 1 - j) if reverse else j
            r0 = pl.multiple_of(blk * SUBLANES, SUBLANES)
            a = a_ref[pl.ds(r0, SUBLANES), :]
            b = b_ref[pl.ds(r0, SUBLANES), :]
            for step in (1, 2, 4):
                if reverse:
                    shift, valid = SUBLANES - step, row < SUBLANES - step
                else:
                    shift, valid = step, row >= step
                a_sh = pltpu.roll(a, shift, axis=0)
                b_sh = pltpu.roll(b, shift, axis=0)
                b = jnp.where(valid, a * b_sh + b, b)
                a = jnp.where(valid, a * a_sh, a)
            h = b + a * carry
            if reverse:
                h_ref[pl.ds(r0, SUBLANES), :] = h_ref[pl.ds(r0, SUBLANES), :] + h
                return h[0:1, :]
            h_ref[pl.ds(r0, SUBLANES), :] = h
            return h[SUBLANES - 1:SUBLANES, :]

        lax.fori_loop(0, n_blocks, body, jnp.zeros((1, cw), F32), unroll=4)

    o_ref[...] = (h_ref[...] * gr_ref[...].astype(F32)).astype(o_ref.dtype)


def _lru(xr, gr_act, conv_w, conv_b, wbd, ba, bx, lam, *, batch, seq, cw):
    t, c = xr.shape
    blk = lambda b, j: (b, j)
    par = lambda b, j: (0, j)
    return pl.pallas_call(
        functools.partial(_lru_kernel, chunk=min(512, seq)),
        grid=(batch, c // cw),
        in_specs=[
            pl.BlockSpec((seq, cw), blk),
            pl.BlockSpec((seq, cw), blk),
            pl.BlockSpec((CONV_WIDTH, cw), par),
            pl.BlockSpec((1, cw), par),
            pl.BlockSpec((2, 2, 1, cw, cw), lambda b, j: (0, 0, j, 0, 0)),
            pl.BlockSpec((2, cw), par),
            pl.BlockSpec((2, cw), par),
            pl.BlockSpec((2, cw), par),
        ],
        out_specs=pl.BlockSpec((seq, cw), blk),
        out_shape=jax.ShapeDtypeStruct((t, c), BF16),
        scratch_shapes=[
            pltpu.VMEM((seq + 2 * SUBLANES, cw), F32),
            pltpu.VMEM((seq, cw), F32),
            pltpu.VMEM((seq, cw), F32),
            pltpu.VMEM((seq, cw), F32),
            pltpu.VMEM((seq, cw), F32),
        ],
        compiler_params=_cparams(2),
        name="conv_rglru",
    )(xr, gr_act, conv_w, conv_b, wbd, ba, bx, lam)


def _merge_kernel(att_ref, hb_ref, ga_ref, gb_ref, h_ref, wpa_ref, wpb_ref, wo_ref, bo_ref,
                  lng_ref, lnb_ref, *rest, alpha, n_experts):
    if n_experts:
        wrh_ref, wrl_ref, br_ref, o_ref, route_ref, counts_ref, cnt_ref = rest
    else:
        (o_ref,) = rest
    branch_a = jnp.dot(att_ref[...], wpa_ref[...], preferred_element_type=F32)
    branch_b = jnp.dot(hb_ref[...], wpb_ref[...], preferred_element_type=F32)
    merged = ga_ref[...].astype(F32) * branch_a + gb_ref[...].astype(F32) * branch_b
    m = jnp.dot(merged.astype(BF16), wo_ref[...], preferred_element_type=F32) + bo_ref[...]
    y = _layernorm(alpha * h_ref[...] + m, lng_ref[...], lnb_ref[...])
    o_ref[...] = y
    if n_experts:
        y_hi = y.astype(BF16)
        y_lo = (y - y_hi.astype(F32)).astype(BF16)
        logits = (jnp.dot(y_hi, wrh_ref[...], preferred_element_type=F32)
                  + jnp.dot(y_lo, wrh_ref[...], preferred_element_type=F32)
                  + jnp.dot(y_hi, wrl_ref[...], preferred_element_type=F32)) + br_ref[...]
        lane = lax.broadcasted_iota(jnp.int32, logits.shape, 1)
        neg = jnp.float32(-jnp.inf)
        lg = jnp.where(lane < n_experts, logits, neg)
        v1 = jnp.max(lg, axis=-1, keepdims=True)
        i1 = jnp.min(jnp.where(lg == v1, lane, LANES), axis=-1, keepdims=True)
        lg2 = jnp.where(lane == i1, neg, lg)
        v2 = jnp.max(lg2, axis=-1, keepdims=True)
        i2 = jnp.min(jnp.where(lg2 == v2, lane, LANES), axis=-1, keepdims=True)
        e21 = jnp.exp(v2 - v1)
        g1 = 1.0 / (1.0 + e21)
        g2 = e21 * g1
        @pl.when(pl.program_id(0) == 0)
        def _():
            cnt_ref[...] = jnp.zeros_like(cnt_ref)

        tm = y.shape[0]
        oh1 = lane == i1
        oh2 = lane == i2
        assign = jnp.where(jnp.logical_or(oh1, oh2), 1.0, 0.0)
        tri = (lax.broadcasted_iota(jnp.int32, (tm, tm), 1)
               < lax.broadcasted_iota(jnp.int32, (tm, tm), 0)).astype(BF16)
        rank = jnp.dot(tri, assign.astype(BF16), preferred_element_type=F32) + cnt_ref[...]
        r1 = jnp.sum(jnp.where(oh1, rank, 0.0), axis=-1, keepdims=True)
        r2 = jnp.sum(jnp.where(oh2, rank, 0.0), axis=-1, keepdims=True)
        cnt_ref[...] += jnp.sum(assign, axis=0, keepdims=True)
        counts_ref[...] = cnt_ref[...]
        cols = (i1.astype(F32), i2.astype(F32), g1, g2, r1, r2)
        route = jnp.zeros_like(logits)
        for n, col in enumerate(cols):
            route = jnp.where(lane == n, col, route)
        route_ref[...] = route


def _merge(att, hb, ga, gb, h, wpa, wpb, wo, bo, ln_g, ln_b, router, *, alpha, tm):
    t, d = h.shape
    aw = att.shape[1]
    lw = hb.shape[1]
    row = lambda i: (i, 0)
    in_specs = [
        pl.BlockSpec((tm, aw), row), pl.BlockSpec((tm, lw), row),
        pl.BlockSpec((tm, d), row), pl.BlockSpec((tm, d), row), pl.BlockSpec((tm, d), row),
        _resident((aw, d)), _resident((lw, d)), _resident((d, d)), _resident((1, d)),
        _resident((1, d)), _resident((1, d)),
    ]
    args = [att, hb, ga, gb, h, wpa, wpb, wo, bo, ln_g, ln_b]
    out_shape = [jax.ShapeDtypeStruct((t, d), F32)]
    out_specs = [pl.BlockSpec((tm, d), row)]
    n_experts = 0
    scratch = []
    if router is not None:
        wr_hi, wr_lo, br, n_experts = router
        in_specs += [_resident((d, LANES)), _resident((d, LANES)), _resident((1, LANES))]
        args += [wr_hi, wr_lo, br]
        out_shape += [jax.ShapeDtypeStruct((t, LANES), F32), jax.ShapeDtypeStruct((1, LANES), F32)]
        out_specs += [pl.BlockSpec((tm, LANES), row), pl.BlockSpec((1, LANES), lambda i: (0, 0))]
        scratch = [pltpu.VMEM((1, LANES), F32)]
    res = pl.pallas_call(
        functools.partial(_merge_kernel, alpha=alpha, n_experts=n_experts),
        grid=(t // tm,),
        in_specs=in_specs,
        out_specs=out_specs,
        out_shape=out_shape,
        scratch_shapes=scratch,
        compiler_params=_cparams(1),
        name="merge_outproj",
    )(*args)
    return res if router is not None else res[0]


def _ffn_kernel(h_ref, wg_ref, wu_ref, wd_ref, lng_ref, lnb_ref, o_ref, *, alpha, f_chunk):
    h = h_ref[...]
    hb = h.astype(BF16)
    d_ff = wg_ref.shape[1]
    f = None
    for lo in range(0, d_ff, f_chunk):
        g = jnp.dot(hb, wg_ref[:, lo:lo + f_chunk], preferred_element_type=F32)
        u = jnp.dot(hb, wu_ref[:, lo:lo + f_chunk], preferred_element_type=F32)
        act = (g * jax.nn.sigmoid(g) * u).astype(BF16)
        part = jnp.dot(act, wd_ref[lo:lo + f_chunk, :], preferred_element_type=F32)
        f = part if f is None else f + part
    o_ref[...] = _layernorm(alpha * h + f, lng_ref[...], lnb_ref[...])


def _ffn(h, wg, wu, wd, ln_g, ln_b, *, alpha, tm, f_chunk):
    t, d = h.shape
    d_ff = wg.shape[1]
    row = lambda i: (i, 0)
    return pl.pallas_call(
        functools.partial(_ffn_kernel, alpha=alpha, f_chunk=f_chunk),
        grid=(t // tm,),
        in_specs=[pl.BlockSpec((tm, d), row), _resident((d, d_ff)), _resident((d, d_ff)),
                  _resident((d_ff, d)), _resident((1, d)), _resident((1, d))],
        out_specs=pl.BlockSpec((tm, d), row),
        out_shape=jax.ShapeDtypeStruct((t, d), F32),
        compiler_params=_cparams(1),
        name="dense_swiglu",
    )(h, wg, wu, wd, ln_g, ln_b)


def _dispatch_kernel(pos_ref, ztile_ref, h_ref, xs_hbm, buf_ref, zero_ref, sems, zsem, *, n_steps, tm_rows):
    i = pl.program_id(0)
    td = h_ref.shape[0]
    slot = i % 2

    @pl.when(i == 0)
    def _():
        zero_ref[...] = jnp.zeros_like(zero_ref)
        parts = tm_rows // td

        def fill(n, c):
            @pl.when(ztile_ref[n] >= 0)
            def _():
                base = ztile_ref[n] * tm_rows
                for part in range(parts):
                    pltpu.make_async_copy(zero_ref, xs_hbm.at[pl.ds(base + part * td, td), :], zsem).start()
            return c
        lax.fori_loop(0, ztile_ref.shape[0], fill, 0)

        def drain(n, c):
            @pl.when(ztile_ref[n] >= 0)
            def _():
                for part in range(parts):
                    pltpu.make_async_copy(zero_ref, xs_hbm.at[pl.ds(0, td), :], zsem).wait()
            return c
        lax.fori_loop(0, ztile_ref.shape[0], drain, 0)

    def wait_slot(s):
        for _ in range(TOP_K):
            pltpu.make_async_copy(buf_ref.at[s], xs_hbm.at[pl.ds(0, td), :], sems.at[s]).wait()

    @pl.when(i >= 2)
    def _():
        wait_slot(slot)

    buf_ref[slot] = h_ref[...]

    def issue(r, c):
        for k in range(TOP_K):
            p = pos_ref[(i * td + r) * TOP_K + k]
            pltpu.make_async_copy(buf_ref.at[slot, pl.ds(r, 1), :], xs_hbm.at[pl.ds(p, 1), :],
                                  sems.at[slot]).start()
        return c
    lax.fori_loop(0, td, issue, 0, unroll=8)

    @pl.when(i == n_steps - 1)
    def _():
        wait_slot(slot)
        if n_steps > 1:
            wait_slot(1 - slot)


def _moe_dispatch(pos_flat, zero_tiles, h, *, n_rows, td, tm_rows):
    t, d = h.shape
    n_steps = t // td
    grid_spec = pltpu.PrefetchScalarGridSpec(
        num_scalar_prefetch=2,
        grid=(n_steps,),
        in_specs=[pl.BlockSpec((td, d), lambda i, pos, zt: (i, 0))],
        out_specs=pl.BlockSpec(memory_space=pl.ANY),
        scratch_shapes=[pltpu.VMEM((2, td, d), F32), pltpu.VMEM((td, d), F32),
                        pltpu.SemaphoreType.DMA((2,)), pltpu.SemaphoreType.DMA(())],
    )
    return pl.pallas_call(
        functools.partial(_dispatch_kernel, n_steps=n_steps, tm_rows=tm_rows),
        grid_spec=grid_spec,
        out_shape=jax.ShapeDtypeStruct((n_rows, d), F32),
        compiler_params=_cparams(1),
        name="moe_dispatch",
    )(pos_flat, zero_tiles, h)


def _moe_kernel(tile_expert_ref, n_active_ref, xs_ref, wg_ref, wu_ref, wd_ref, o_ref, xb_ref, acc_ref):
    i = pl.program_id(0)
    j = pl.program_id(1)
    nj = pl.num_programs(1)
    active = i < n_active_ref[0]

    @pl.when(jnp.logical_and(active, j == 0))
    def _():
        xb_ref[...] = xs_ref[...].astype(BF16)

    @pl.when(active)
    def _():
        xb = xb_ref[...]
        g = jnp.dot(xb, wg_ref[0], preferred_element_type=F32)
        u = jnp.dot(xb, wu_ref[0], preferred_element_type=F32)
        act = (g * jax.nn.sigmoid(g) * u).astype(BF16)
        part = jnp.dot(act, wd_ref[0], preferred_element_type=F32)

        @pl.when(j == 0)
        def _():
            acc_ref[...] = part

        @pl.when(j > 0)
        def _():
            acc_ref[...] += part

    @pl.when(j == nj - 1)
    def _():
        o_ref[...] = jnp.where(active, acc_ref[...], 0.0)


def _moe_experts(tile_expert, n_active, xs, wg, wu, wd, *, tm, tf):
    n_rows, d = xs.shape
    d_ff = wg.shape[2]
    grid_spec = pltpu.PrefetchScalarGridSpec(
        num_scalar_prefetch=2,
        grid=(n_rows // tm, d_ff // tf),
        in_specs=[
            pl.BlockSpec((tm, d), lambda i, j, te, na: (jnp.minimum(i, na[0] - 1), 0)),
            pl.BlockSpec((1, d, tf), lambda i, j, te, na: (te[i], 0, j)),
            pl.BlockSpec((1, d, tf), lambda i, j, te, na: (te[i], 0, j)),
            pl.BlockSpec((1, tf, d), lambda i, j, te, na: (te[i], j, 0)),
        ],
        out_specs=pl.BlockSpec((tm, d), lambda i, j, te, na: (i, 0)),
        scratch_shapes=[pltpu.VMEM((tm, d), BF16), pltpu.VMEM((tm, d), F32)],
    )
    return pl.pallas_call(
        _moe_kernel,
        grid_spec=grid_spec,
        out_shape=jax.ShapeDtypeStruct((n_rows, d), F32),
        compiler_params=_cparams(2),
        name="moe_experts",
    )(tile_expert, n_active, xs, wg, wu, wd)


def _combine_kernel(pos_ref, y_hbm, route_ref, h_ref, lng_ref, lnb_ref, o_ref, ybuf_ref, sems, *, alpha, n_steps):
    i = pl.program_id(0)
    tm = h_ref.shape[0]
    n_rows = TOP_K * tm
    slot = i % 2

    def issue(step, s):
        def body(r, c):
            p = pos_ref[step * n_rows + r]
            pltpu.make_async_copy(y_hbm.at[pl.ds(p, 1), :], ybuf_ref.at[s, pl.ds(r, 1), :], sems.at[s]).start()
            return c
        lax.fori_loop(0, n_rows, body, 0, unroll=16)

    @pl.when(i == 0)
    def _():
        issue(0, 0)

    @pl.when(i + 1 < n_steps)
    def _():
        issue(i + 1, 1 - slot)

    pltpu.make_async_copy(y_hbm.at[pl.ds(0, n_rows), :], ybuf_ref.at[slot], sems.at[slot]).wait()
    f = None
    for k in range(TOP_K):
        term = route_ref[:, TOP_K + k:TOP_K + k + 1] * ybuf_ref[slot, k * tm:(k + 1) * tm, :]
        f = term if f is None else f + term
    o_ref[...] = _layernorm(alpha * h_ref[...] + f, lng_ref[...], lnb_ref[...])


def _moe_combine(pos_tiles, y_sorted, route, h, ln_g, ln_b, *, alpha, tm):
    t, d = h.shape
    n_steps = t // tm
    grid_spec = pltpu.PrefetchScalarGridSpec(
        num_scalar_prefetch=1,
        grid=(n_steps,),
        in_specs=[
            pl.BlockSpec(memory_space=pl.ANY),
            pl.BlockSpec((tm, LANES), lambda i, pos: (i, 0)),
            pl.BlockSpec((tm, d), lambda i, pos: (i, 0)),
            pl.BlockSpec((1, d), lambda i, pos: (0, 0)),
            pl.BlockSpec((1, d), lambda i, pos: (0, 0)),
        ],
        out_specs=pl.BlockSpec((tm, d), lambda i, pos: (i, 0)),
        scratch_shapes=[pltpu.VMEM((2, TOP_K * tm, d), F32), pltpu.SemaphoreType.DMA((2,))],
    )
    return pl.pallas_call(
        functools.partial(_combine_kernel, alpha=alpha, n_steps=n_steps),
        grid_spec=grid_spec,
        out_shape=jax.ShapeDtypeStruct((t, d), F32),
        compiler_params=_cparams(1),
        name="moe_combine",
    )(pos_tiles, y_sorted, route, h, ln_g, ln_b)


def _route_plan(route, counts, n_experts, tm_rows, tm_comb):
    t = route.shape[0]
    experts = route[:, 0:TOP_K].astype(jnp.int32)
    ranks = route[:, 2 * TOP_K:3 * TOP_K].astype(jnp.int32)
    counts = counts[0, :n_experts].astype(jnp.int32)
    padded = ((counts + tm_rows - 1) // tm_rows) * tm_rows
    ends = jnp.cumsum(padded)
    starts = ends - padded
    onehot = (experts[:, :, None] == jnp.arange(n_experts, dtype=jnp.int32)).astype(jnp.int32)
    pos = jnp.sum(onehot * starts, axis=-1) + ranks
    n_rows = t * TOP_K + n_experts * tm_rows
    n_tiles = n_rows // tm_rows
    n_active = (ends[-1] // tm_rows).astype(jnp.int32)
    tile_start = jnp.arange(n_tiles, dtype=jnp.int32) * tm_rows
    tile_expert = jnp.sum((tile_start[:, None] >= ends[None, :]).astype(jnp.int32), axis=1)
    last_expert = jnp.sum((ends[-1] - 1 >= ends).astype(jnp.int32))
    tile_expert = jnp.minimum(tile_expert, last_expert).astype(jnp.int32)
    last_tile = jnp.where(padded > 0, ends // tm_rows - 1, -1)
    tail_tile = n_active + jnp.arange(n_experts, dtype=jnp.int32)
    tail_tile = jnp.where(tail_tile < n_tiles, tail_tile, -1)
    zero_tiles = jnp.concatenate([last_tile, tail_tile]).astype(jnp.int32)
    pos_tiles = pos.reshape(t // tm_comb, tm_comb, TOP_K).transpose(0, 2, 1).reshape(-1)
    return tile_expert, n_active.reshape(1), zero_tiles, pos.reshape(-1), pos_tiles, n_rows


def _pick(n, pref):
    t = min(n, pref)
    assert n % t == 0, (n, pref)
    return t


def kernel(x, ln_in_g, ln_in_b, w_in, b_in, lam_q1, lam_k1, lam_q2, lam_k2, subln_g, conv_w, conv_b, rg_wa, rg_ba, rg_wx, rg_bx, rg_lam, w_pa, w_pb, w_o, b_o, ln1_g, ln1_b, ffn_wg, ffn_wu, ffn_wd, moe_wr, moe_br, moe_wg, moe_wu, moe_wd, ln2_g, ln2_b):
    batch, seq, d = x.shape
    depth = w_in.shape[0]
    t = batch * seq
    attn_w = w_pa.shape[1]
    lru_w = w_pb.shape[1]
    qk_w = (w_in.shape[2] - attn_w - 2 * lru_w - 2 * d) // 2
    n_experts = moe_wr.shape[-1]
    alpha = (2.0 * depth) ** 0.25

    tm_proj = _pick(seq, 512)
    tq = _pick(seq, 2048)
    tm_merge = _pick(t, 512)
    tm_ffn = _pick(t, 512)
    tm_moe = _pick(t, 512)
    tm_comb = _pick(t, 512)
    cw = MXU_DIM

    half = HEAD_DIM // 2
    inv = 1.0 / (ROPE_THETA ** (jnp.arange(half, dtype=F32) * (2.0 / HEAD_DIM)))
    ang = jnp.arange(seq, dtype=jnp.int32).astype(F32)[:, None] * inv[None, :]
    cos_t = jnp.tile(jnp.cos(ang), (1, LANES // half))
    sin_t = jnp.tile(jnp.concatenate([-jnp.sin(ang), jnp.sin(ang)], axis=-1), (1, LANES // HEAD_DIM))

    row2 = lambda v: v.reshape(1, -1)
    h = x.reshape(t, d)
    w_in_next = None
    for l in range(depth):
        lambda_init = 0.8 - 0.6 * math.exp(-0.3 * l)
        lam = (jnp.exp(jnp.sum(lam_q1[l] * lam_k1[l])) - jnp.exp(jnp.sum(lam_q2[l] * lam_k2[l]))
               + lambda_init).reshape(1).astype(F32)
        w_in_l = w_in[l].astype(BF16) if w_in_next is None else w_in_next[0]
        outs = _inproj(h, row2(ln_in_g), row2(ln_in_b), w_in_l, row2(b_in[l]), cos_t, sin_t,
                       apply_ln=(l == 0), batch=batch, seq=seq, qk_w=qk_w, v_w=attn_w, lru_w=lru_w, tm=tm_proj)
        if l == 0:
            h, *outs = outs
        qt, kk, vt, xr, gr_act, ga, gb = outs

        is_moe = l % 2 == 1
        j = l // 2
        side = [moe_wg[j], moe_wu[j], moe_wd[j]] if is_moe else [ffn_wg[j:j + 1], ffn_wu[j:j + 1]]
        if l + 1 < depth:
            side.append(w_in[l + 1:l + 2])
        att, side = _attention(lam, qt, kk, vt, row2(subln_g[l]), out_scale=1.0 - lambda_init, tq=tq,
                               to_bf16=tuple(side))
        w_in_next = side[-1] if l + 1 < depth else None
        expert_w = side[:3] if is_moe else ()

        nb = cw // LRU_BLOCK
        def dense_tiles(w):
            w = w.reshape(2, lru_w // cw, nb, LRU_BLOCK, LRU_BLOCK)
            eye = jnp.eye(nb, dtype=w.dtype)
            return jnp.einsum('dtncm,nk->dtnckm', w, eye).reshape(2, lru_w // cw, cw, cw)
        wbd = jnp.stack([dense_tiles(rg_wa[l]), dense_tiles(rg_wx[l])], axis=1).astype(BF16)
        hb = _lru(xr, gr_act, conv_w[l], row2(conv_b[l]), wbd, rg_ba[l], rg_bx[l], rg_lam[l],
                  batch=batch, seq=seq, cw=cw)

        is_moe = l % 2 == 1
        router = None
        if is_moe:
            j = l // 2
            wr = jnp.zeros((d, LANES), F32).at[:, :n_experts].set(moe_wr[j])
            br = jnp.zeros((1, LANES), F32).at[0, :n_experts].set(moe_br[j])
            wr_hi = wr.astype(BF16)
            wr_lo = (wr - wr_hi.astype(F32)).astype(BF16)
            router = (wr_hi, wr_lo, br, n_experts)
        res = _merge(att, hb, ga, gb, h, w_pa[l].astype(BF16), w_pb[l].astype(BF16), w_o[l].astype(BF16),
                     row2(b_o[l]), row2(ln1_g[l]), row2(ln1_b[l]), router, alpha=alpha, tm=tm_merge)
        if not is_moe:
            h = res
            j = l // 2
            d_ff = ffn_wg.shape[2]
            f_chunk = d_ff // 2 if (d_ff // 2) % LANES == 0 else d_ff
            h = _ffn(h, side[0][0], side[1][0], ffn_wd[j].astype(BF16),
                     row2(ln2_g[l]), row2(ln2_b[l]), alpha=alpha, tm=tm_ffn, f_chunk=f_chunk)
        else:
            h, route, counts = res
            d_ff = moe_wg.shape[3]
            tf = d_ff // 2 if (d_ff // 2) % LANES == 0 else d_ff
            tile_expert, n_active, zero_tiles, pos_flat, pos_tiles, n_rows = _route_plan(
                route, counts, n_experts, tm_moe, tm_comb)
            xs = _moe_dispatch(pos_flat, zero_tiles, h, n_rows=n_rows, td=tm_comb, tm_rows=tm_moe)
            y_sorted = _moe_experts(tile_expert, n_active, xs,
                                    *expert_w,
                                    tm=tm_moe, tf=tf)
            h = _moe_combine(pos_tiles, y_sorted, route, h, row2(ln2_g[l]), row2(ln2_b[l]),
                             alpha=alpha, tm=tm_comb)
    return h.reshape(batch, seq, d)
```

```python
import functools
import math

import jax
import jax.numpy as jnp
from jax import lax
from jax.experimental import pallas as pl
from jax.experimental.pallas import tpu as pltpu

F32 = jnp.float32
BF16 = jnp.bfloat16

HEAD_DIM = 64
V_HEAD_DIM = 2 * HEAD_DIM
LANES = 128
SUBLANES = 8
MXU_DIM = 256
ROPE_THETA = 10000.0
LRU_BLOCK = 64
LRU_C = 8.0
CONV_WIDTH = 4
TOP_K = 2
LN_EPS = 1e-5
VMEM_LIMIT = 52 * 1024 * 1024


def _cparams(n_axes, vmem=VMEM_LIMIT):
    return pltpu.CompilerParams(dimension_semantics=("arbitrary",) * n_axes, vmem_limit_bytes=vmem)


def _layernorm(x, g, b):
    mu = jnp.mean(x, axis=-1, keepdims=True)
    xc = x - mu
    var = jnp.mean(xc * xc, axis=-1, keepdims=True)
    return xc * lax.rsqrt(var + LN_EPS) * g + b


def _gelu_tanh(x):
    return 0.5 * x * (1.0 + jnp.tanh(math.sqrt(2.0 / math.pi) * (x + 0.044715 * (x * x * x))))


def _resident(shape):
    nd = len(shape)
    return pl.BlockSpec(shape, lambda *_: (0,) * nd, pipeline_mode=pl.Buffered(1))


def _inproj_kernel(x_ref, lng_ref, lnb_ref, w_ref, b_ref, cos_ref, sin_ref, *outs,
                   apply_ln, qk_w, v_w, lru_w, d_model):
    if apply_ln:
        h_ref, qt_ref, k_ref, vt_ref, xr_ref, gr_ref, ga_ref, gb_ref = outs
    else:
        qt_ref, k_ref, vt_ref, xr_ref, gr_ref, ga_ref, gb_ref = outs
    x = x_ref[...]
    if apply_ln:
        x = _layernorm(x, lng_ref[...], lnb_ref[...])
        h_ref[...] = x
    hb = x.astype(BF16)
    tm = x.shape[0]

    def proj(lo, width):
        return jnp.dot(hb, w_ref[:, lo:lo + width], preferred_element_type=F32) + b_ref[:, lo:lo + width]

    cos = cos_ref[...]
    sin = sin_ref[...]
    lane = lax.broadcasted_iota(jnp.int32, (tm, LANES), 1)
    first_half = (lane % HEAD_DIM) < (HEAD_DIM // 2)

    def rope(t):
        swapped = jnp.where(first_half,
                            pltpu.roll(t, LANES - HEAD_DIM // 2, axis=1),
                            pltpu.roll(t, HEAD_DIM // 2, axis=1))
        return t * cos + swapped * sin

    off = 0
    q = proj(off, qk_w)
    off += qk_w
    scale = HEAD_DIM ** -0.5 * math.log2(math.e)
    for c in range(qk_w // LANES):
        r = rope(q[:, c * LANES:(c + 1) * LANES]) * scale
        qt_ref[0, c * LANES:(c + 1) * LANES, :] = r.T.astype(BF16)
    k = proj(off, qk_w)
    off += qk_w
    for c in range(qk_w // LANES):
        k_ref[:, c * LANES:(c + 1) * LANES] = rope(k[:, c * LANES:(c + 1) * LANES]).astype(BF16)
    v = proj(off, v_w)
    off += v_w
    for c in range(v_w // LANES):
        vt_ref[0, c * LANES:(c + 1) * LANES, :] = v[:, c * LANES:(c + 1) * LANES].T.astype(BF16)
    xr_ref[...] = proj(off, lru_w)
    off += lru_w
    gr_ref[...] = _gelu_tanh(proj(off, lru_w)).astype(BF16)
    off += lru_w
    ga_ref[...] = jax.nn.sigmoid(proj(off, d_model)).astype(BF16)
    off += d_model
    gb_ref[...] = jax.nn.sigmoid(proj(off, d_model)).astype(BF16)


def _inproj(x2d, ln_g, ln_b, w, b, cos_t, sin_t, *, apply_ln, batch, seq, qk_w, v_w, lru_w, tm):
    t, d = x2d.shape
    n_cols = w.shape[1]
    spt = seq // tm
    row = lambda i: (i, 0)
    tr = lambda i: (i // spt, 0, i % spt)
    out_shape = [
        jax.ShapeDtypeStruct((batch, qk_w, seq), BF16),
        jax.ShapeDtypeStruct((t, qk_w), BF16),
        jax.ShapeDtypeStruct((batch, v_w, seq), BF16),
        jax.ShapeDtypeStruct((t, lru_w), F32),
        jax.ShapeDtypeStruct((t, lru_w), BF16),
        jax.ShapeDtypeStruct((t, d), BF16),
        jax.ShapeDtypeStruct((t, d), BF16),
    ]
    out_specs = [
        pl.BlockSpec((1, qk_w, tm), tr),
        pl.BlockSpec((tm, qk_w), row),
        pl.BlockSpec((1, v_w, tm), tr),
        pl.BlockSpec((tm, lru_w), row),
        pl.BlockSpec((tm, lru_w), row),
        pl.BlockSpec((tm, d), row),
        pl.BlockSpec((tm, d), row),
    ]
    if apply_ln:
        out_shape = [jax.ShapeDtypeStruct((t, d), F32)] + out_shape
        out_specs = [pl.BlockSpec((tm, d), row)] + out_specs
    kern = functools.partial(_inproj_kernel, apply_ln=apply_ln, qk_w=qk_w, v_w=v_w, lru_w=lru_w, d_model=d)
    return pl.pallas_call(
        kern,
        grid=(t // tm,),
        in_specs=[
            pl.BlockSpec((tm, d), row),
            _resident((1, d)), _resident((1, d)),
            _resident((d, n_cols)), _resident((1, n_cols)),
            pl.BlockSpec((tm, LANES), lambda i: (i % spt, 0)),
            pl.BlockSpec((tm, LANES), lambda i: (i % spt, 0)),
        ],
        out_specs=out_specs,
        out_shape=out_shape,
        compiler_params=_cparams(1),
        name="inproj",
    )(x2d, ln_g, ln_b, w, b, cos_t, sin_t)


def _attn_kernel(lam_ref, qt_ref, k_ref, vt_ref, g_ref, *rest, out_scale, chunk, qw, lag, n_cast):
    cast_in = rest[:n_cast]
    o_ref = rest[n_cast]
    cast_out = rest[n_cast + 1:2 * n_cast + 1]
    vaug_ref, s1_ref, s2_ref = rest[2 * n_cast + 1:]
    i = pl.program_id(2)
    n_ones = vaug_ref.shape[0] - V_HEAD_DIM

    for src, dst in zip(cast_in, cast_out):
        dst[...] = src[...].astype(dst.dtype)

    @pl.when(i == 0)
    def _():
        vaug_ref[0:V_HEAD_DIM, :] = vt_ref[0]
        vaug_ref[V_HEAD_DIM:, :] = jnp.ones((n_ones, vaug_ref.shape[1]), BF16)

    seq = k_ref.shape[0]
    tq = qt_ref.shape[2]
    n_chunks = seq // chunk
    row = lax.broadcasted_iota(jnp.int32, (V_HEAD_DIM, qw), 0)
    units = [(qb, mp) for qb in range(tq // qw) for mp in range(2)]
    bufs = (s1_ref, s2_ref)

    def q_unit(u):
        qb, mp = units[u]
        q = qt_ref[0, :, qb * qw:(qb + 1) * qw]
        keep = (row < HEAD_DIM) if mp == 0 else (row >= HEAD_DIM)
        return jnp.where(keep, q, jnp.zeros_like(q))

    shift = n_chunks + lag
    assert shift < 2 * n_chunks
    n_tasks = len(units) * n_chunks
    q_cur, m_run, m_chunk, acc, outs = None, None, {}, None, []
    for t in range(n_tasks + shift):
        if t < n_tasks:
            u, c = divmod(t, n_chunks)
            if c == 0:
                q_cur, m_run = q_unit(u), None
            s = jnp.dot(k_ref[c * chunk:(c + 1) * chunk, :], q_cur, preferred_element_type=F32)
            mc = jnp.max(s, axis=0, keepdims=True)
            m_run = mc if m_run is None else jnp.maximum(m_run, mc)
            m_chunk[u, c] = m_run
            bufs[u % 2][c * chunk:(c + 1) * chunk, :] = jnp.exp2((s - m_run).astype(BF16))
        if t >= shift:
            u, c = divmod(t - shift, n_chunks)
            e = bufs[u % 2][c * chunk:(c + 1) * chunk, :]
            part = jnp.dot(vaug_ref[:, c * chunk:(c + 1) * chunk], e, preferred_element_type=F32)
            part = part * jnp.exp2(m_chunk[u, c] - m_chunk[u, n_chunks - 1])
            acc = part if c == 0 else acc + part
            if c == n_chunks - 1:
                outs.append(acc[0:V_HEAD_DIM, :] * (1.0 / acc[V_HEAD_DIM:V_HEAD_DIM + 1, :]))

    for qb in range(tq // qw):
        o = (outs[2 * qb] - lam_ref[0] * outs[2 * qb + 1]).T
        ms = jnp.mean(o * o, axis=-1, keepdims=True)
        o_ref[qb * qw:(qb + 1) * qw, :] = (o * lax.rsqrt(ms + LN_EPS) * g_ref[...] * out_scale).astype(o_ref.dtype)


def _attention(lam, qt, k, vt, subln_g, *, out_scale, tq, to_bf16=()):
    batch, qk_w, seq = qt.shape
    heads = qk_w // V_HEAD_DIM
    nq = seq // tq
    n_steps = batch * heads * nq
    n_ones = 16
    chunk = _pick(seq, 256)
    kern = functools.partial(_attn_kernel, out_scale=out_scale, chunk=chunk, qw=MXU_DIM,
                             lag=min(4, seq // chunk - 1), n_cast=len(to_bf16))
    cast_in_specs, cast_out_specs, cast_shapes, cast_args = [], [], [], []
    for entry in to_bf16:
        w, e0 = entry if isinstance(entry, tuple) else (entry, None)
        n_e, n_r, n_c = (1 if e0 is not None else w.shape[0]), w.shape[1], w.shape[2]
        per = n_e * n_r // n_steps
        assert per * n_steps == n_e * n_r and n_r % per == 0 and per % 16 == 0, (w.shape, n_steps)
        per_e = n_r // per

        def index(b, h, i, per_e=per_e, e0=0):
            s = (b * heads + h) * nq + i
            return (s // per_e + e0, s % per_e, 0)
        cast_in_specs.append(pl.BlockSpec((1, per, n_c), functools.partial(index, e0=e0 or 0)))
        cast_out_specs.append(pl.BlockSpec((1, per, n_c), index))
        cast_shapes.append(jax.ShapeDtypeStruct((n_e, n_r, n_c), BF16))
        cast_args.append(w)
    res = pl.pallas_call(
        kern,
        grid=(batch, heads, nq),
        in_specs=[
            pl.BlockSpec(memory_space=pltpu.SMEM),
            pl.BlockSpec((1, V_HEAD_DIM, tq), lambda b, h, i: (b, h, i)),
            pl.BlockSpec((seq, V_HEAD_DIM), lambda b, h, i: (b, h)),
            pl.BlockSpec((1, V_HEAD_DIM, seq), lambda b, h, i: (b, h, 0)),
            pl.BlockSpec((1, V_HEAD_DIM), lambda b, h, i: (0, 0)),
        ] + cast_in_specs,
        out_specs=[pl.BlockSpec((tq, V_HEAD_DIM), lambda b, h, i: (b * nq + i, h))] + cast_out_specs,
        out_shape=[jax.ShapeDtypeStruct((batch * seq, qk_w), BF16)] + cast_shapes,
        scratch_shapes=[pltpu.VMEM((V_HEAD_DIM + n_ones, seq), BF16),
                        pltpu.VMEM((seq, MXU_DIM), BF16), pltpu.VMEM((seq, MXU_DIM), BF16)],
        compiler_params=_cparams(3),
        name="diff_attn",
    )(lam, qt, k, vt, subln_g, *cast_args)
    return res[0], tuple(res[1:])


def _lru_kernel(xr_ref, gr_ref, cw_ref, cb_ref, wbd_ref, ba_ref, bx_ref, lam_ref, o_ref,
                xpad_ref, xc_ref, a_ref, b_ref, h_ref, *, chunk):
    seq, cw = xr_ref.shape
    pad = SUBLANES
    zeros = jnp.zeros((pad, cw), F32)
    xpad_ref[0:pad, :] = zeros
    xpad_ref[pad + seq:, :] = zeros
    xpad_ref[pad:pad + seq, :] = xr_ref[...]
    n_chunks = seq // chunk

    for c in range(n_chunks):
        lo = c * chunk
        acc = cb_ref[...] + cw_ref[0:1, :] * xpad_ref[pad + lo - 2:pad + lo - 2 + chunk, :]
        for tap in range(1, CONV_WIDTH):
            s0 = pad + lo - 2 + tap
            acc = acc + cw_ref[tap:tap + 1, :] * xpad_ref[s0:s0 + chunk, :]
        xc_ref[lo:lo + chunk, :] = acc

    row = lax.broadcasted_iota(jnp.int32, (SUBLANES, cw), 0)
    n_blocks = seq // SUBLANES

    for d in range(2):
        neg_lam = -lam_ref[d:d + 1, :]
        softplus = jnp.maximum(neg_lam, 0.0) + jnp.log1p(jnp.exp(-jnp.abs(neg_lam)))
        for c in range(n_chunks):
            lo = c * chunk
            xc = xc_ref[lo:lo + chunk, :]
            xb = xc.astype(BF16)
            zr = jnp.dot(xb, wbd_ref[d, 0, 0], preferred_element_type=F32) + ba_ref[d:d + 1, :]
            zg = jnp.dot(xb, wbd_ref[d, 1, 0], preferred_element_type=F32) + bx_ref[d:d + 1, :]
            r = 0.5 + 0.5 * jnp.tanh(0.5 * zr)
            g = 0.5 + 0.5 * jnp.tanh(0.5 * zg)
            log_a = -LRU_C * r * softplus
            a_ref[lo:lo + chunk, :] = jnp.exp(log_a)
            th = jnp.tanh(log_a)
            u = -2.0 * th
            root = jnp.where(u > 0.0, u * lax.rsqrt(u * (1.0 - th)), 0.0)
            b_ref[lo:lo + chunk, :] = root * (g * xc)

        reverse = d == 1

        def body(j, carry):
            blk = (n_blocks - 1 - j) if reverse else j
            r0 = pl.multiple_of(blk * SUBLANES, SUBLANES)
            a = a_ref[pl.ds(r0, SUBLANES), :]
            b = b_ref[pl.ds(r0, SUBLANES), :]
            for step in (1, 2, 4):
                if reverse:
                    shift, valid = SUBLANES - step, row < SUBLANES - step
                else:
                    shift, valid = step, row >= step
                a_sh = pltpu.roll(a, shift, axis=0)
                b_sh = pltpu.roll(b, shift, axis=0)
                b = jnp.where(valid, a * b_sh + b, b)
                a = jnp.where(valid, a * a_sh, a)
            h = b + a * carry
            if reverse:
                h_ref[pl.ds(r0, SUBLANES), :] = h_ref[pl.ds(r0, SUBLANES), :] + h
                return h[0:1, :]
            h_ref[pl.ds(r0, SUBLANES), :] = h
            return h[SUBLANES - 1:SUBLANES, :]

        lax.fori_loop(0, n_blocks, body, jnp.zeros((1, cw), F32), unroll=4)

    o_ref[...] = (h_ref[...] * gr_ref[...].astype(F32)).astype(o_ref.dtype)


def _lru(xr, gr_act, conv_w, conv_b, wbd, ba, bx, lam, *, batch, seq, cw):
    t, c = xr.shape
    blk = lambda b, j: (b, j)
    par = lambda b, j: (0, j)
    return pl.pallas_call(
        functools.partial(_lru_kernel, chunk=min(512, seq)),
        grid=(batch, c // cw),
        in_specs=[
            pl.BlockSpec((seq, cw), blk),
            pl.BlockSpec((seq, cw), blk),
            pl.BlockSpec((CONV_WIDTH, cw), par),
            pl.BlockSpec((1, cw), par),
            pl.BlockSpec((2, 2, 1, cw, cw), lambda b, j: (0, 0, j, 0, 0)),
            pl.BlockSpec((2, cw), par),
            pl.BlockSpec((2, cw), par),
            pl.BlockSpec((2, cw), par),
        ],
        out_specs=pl.BlockSpec((seq, cw), blk),
        out_shape=jax.ShapeDtypeStruct((t, c), BF16),
        scratch_shapes=[
            pltpu.VMEM((seq + 2 * SUBLANES, cw), F32),
            pltpu.VMEM((seq, cw), F32),
            pltpu.VMEM((seq, cw), F32),
            pltpu.VMEM((seq, cw), F32),
            pltpu.VMEM((seq, cw), F32),
        ],
        compiler_params=_cparams(2),
        name="conv_rglru",
    )(xr, gr_act, conv_w, conv_b, wbd, ba, bx, lam)


def _merge_kernel(att_ref, hb_ref, ga_ref, gb_ref, h_ref, wpa_ref, wpb_ref, wo_ref, bo_ref,
                  lng_ref, lnb_ref, *rest, alpha, n_experts):
    if n_experts:
        wrh_ref, wrl_ref, br_ref, o_ref, route_ref, counts_ref, cnt_ref = rest
    else:
        (o_ref,) = rest
    branch_a = jnp.dot(att_ref[...], wpa_ref[...], preferred_element_type=F32)
    branch_b = jnp.dot(hb_ref[...], wpb_ref[...], preferred_element_type=F32)
    merged = ga_ref[...].astype(F32) * branch_a + gb_ref[...].astype(F32) * branch_b
    m = jnp.dot(merged.astype(BF16), wo_ref[...], preferred_element_type=F32) + bo_ref[...]
    y = _layernorm(alpha * h_ref[...] + m, lng_ref[...], lnb_ref[...])
    o_ref[...] = y
    if n_experts:
        y_hi = y.astype(BF16)
        y_lo = (y - y_hi.astype(F32)).astype(BF16)
        logits = (jnp.dot(y_hi, wrh_ref[...], preferred_element_type=F32)
                  + jnp.dot(y_lo, wrh_ref[...], preferred_element_type=F32)
                  + jnp.dot(y_hi, wrl_ref[...], preferred_element_type=F32)) + br_ref[...]
        lane = lax.broadcasted_iota(jnp.int32, logits.shape, 1)
        neg = jnp.float32(-jnp.inf)
        lg = jnp.where(lane < n_experts, logits, neg)
        v1 = jnp.max(lg, axis=-1, keepdims=True)
        i1 = jnp.min(jnp.where(lg == v1, lane, LANES), axis=-1, keepdims=True)
        lg2 = jnp.where(lane == i1, neg, lg)
        v2 = jnp.max(lg2, axis=-1, keepdims=True)
        i2 = jnp.min(jnp.where(lg2 == v2, lane, LANES), axis=-1, keepdims=True)
        e21 = jnp.exp(v2 - v1)
        g1 = 1.0 / (1.0 + e21)
        g2 = e21 * g1
        @pl.when(pl.program_id(0) == 0)
        def _():
            cnt_ref[...] = jnp.zeros_like(cnt_ref)

        tm = y.shape[0]
        oh1 = lane == i1
        oh2 = lane == i2
        assign = jnp.where(jnp.logical_or(oh1, oh2), 1.0, 0.0)
        tri = (lax.broadcasted_iota(jnp.int32, (tm, tm), 1)
               < lax.broadcasted_iota(jnp.int32, (tm, tm), 0)).astype(BF16)
        rank = jnp.dot(tri, assign.astype(BF16), preferred_element_type=F32) + cnt_ref[...]
        r1 = jnp.sum(jnp.where(oh1, rank, 0.0), axis=-1, keepdims=True)
        r2 = jnp.sum(jnp.where(oh2, rank, 0.0), axis=-1, keepdims=True)
        cnt_ref[...] += jnp.sum(assign, axis=0, keepdims=True)
        counts_ref[...] = cnt_ref[...]
        cols = (i1.astype(F32), i2.astype(F32), g1, g2, r1, r2)
        route = jnp.zeros_like(logits)
        for n, col in enumerate(cols):
            route = jnp.where(lane == n, col, route)
        route_ref[...] = route


def _merge(att, hb, ga, gb, h, wpa, wpb, wo, bo, ln_g, ln_b, router, *, alpha, tm):
    t, d = h.shape
    aw = att.shape[1]
    lw = hb.shape[1]
    row = lambda i: (i, 0)
    in_specs = [
        pl.BlockSpec((tm, aw), row), pl.BlockSpec((tm, lw), row),
        pl.BlockSpec((tm, d), row), pl.BlockSpec((tm, d), row), pl.BlockSpec((tm, d), row),
        _resident((aw, d)), _resident((lw, d)), _resident((d, d)), _resident((1, d)),
        _resident((1, d)), _resident((1, d)),
    ]
    args = [att, hb, ga, gb, h, wpa, wpb, wo, bo, ln_g, ln_b]
    out_shape = [jax.ShapeDtypeStruct((t, d), F32)]
    out_specs = [pl.BlockSpec((tm, d), row)]
    n_experts = 0
    scratch = []
    if router is not None:
        wr_hi, wr_lo, br, n_experts = router
        in_specs += [_resident((d, LANES)), _resident((d, LANES)), _resident((1, LANES))]
        args += [wr_hi, wr_lo, br]
        out_shape += [jax.ShapeDtypeStruct((t, LANES), F32), jax.ShapeDtypeStruct((1, LANES), F32)]
        out_specs += [pl.BlockSpec((tm, LANES), row), pl.BlockSpec((1, LANES), lambda i: (0, 0))]
        scratch = [pltpu.VMEM((1, LANES), F32)]
    res = pl.pallas_call(
        functools.partial(_merge_kernel, alpha=alpha, n_experts=n_experts),
        grid=(t // tm,),
        in_specs=in_specs,
        out_specs=out_specs,
        out_shape=out_shape,
        scratch_shapes=scratch,
        compiler_params=_cparams(1),
        name="merge_outproj",
    )(*args)
    return res if router is not None else res[0]


def _ffn_kernel(h_ref, wg_ref, wu_ref, wd_ref, lng_ref, lnb_ref, o_ref, *, alpha, f_chunk):
    h = h_ref[...]
    hb = h.astype(BF16)
    d_ff = wg_ref.shape[1]
    f = None
    for lo in range(0, d_ff, f_chunk):
        g = jnp.dot(hb, wg_ref[:, lo:lo + f_chunk], preferred_element_type=F32)
        u = jnp.dot(hb, wu_ref[:, lo:lo + f_chunk], preferred_element_type=F32)
        act = (g * jax.nn.sigmoid(g) * u).astype(BF16)
        part = jnp.dot(act, wd_ref[lo:lo + f_chunk, :], preferred_element_type=F32)
        f = part if f is None else f + part
    o_ref[...] = _layernorm(alpha * h + f, lng_ref[...], lnb_ref[...])


def _ffn(h, wg, wu, wd, ln_g, ln_b, *, alpha, tm, f_chunk):
    t, d = h.shape
    d_ff = wg.shape[1]
    row = lambda i: (i, 0)
    return pl.pallas_call(
        functools.partial(_ffn_kernel, alpha=alpha, f_chunk=f_chunk),
        grid=(t // tm,),
        in_specs=[pl.BlockSpec((tm, d), row), _resident((d, d_ff)), _resident((d, d_ff)),
                  _resident((d_ff, d)), _resident((1, d)), _resident((1, d))],
        out_specs=pl.BlockSpec((tm, d), row),
        out_shape=jax.ShapeDtypeStruct((t, d), F32),
        compiler_params=_cparams(1),
        name="dense_swiglu",
    )(h, wg, wu, wd, ln_g, ln_b)


def _dispatch_kernel(pos_ref, ztile_ref, h_ref, xs_hbm, buf_ref, zero_ref, sems, zsem, *, n_steps, tm_rows):
    i = pl.program_id(0)
    td = h_ref.shape[0]
    slot = i % 2

    @pl.when(i == 0)
    def _():
        zero_ref[...] = jnp.zeros_like(zero_ref)
        parts = tm_rows // td

        def fill(n, c):
            @pl.when(ztile_ref[n] >= 0)
            def _():
                base = ztile_ref[n] * tm_rows
                for part in range(parts):
                    pltpu.make_async_copy(zero_ref, xs_hbm.at[pl.ds(base + part * td, td), :], zsem).start()
            return c
        lax.fori_loop(0, ztile_ref.shape[0], fill, 0)

        def drain(n, c):
            @pl.when(ztile_ref[n] >= 0)
            def _():
                for part in range(parts):
                    pltpu.make_async_copy(zero_ref, xs_hbm.at[pl.ds(0, td), :], zsem).wait()
            return c
        lax.fori_loop(0, ztile_ref.shape[0], drain, 0)

    def wait_slot(s):
        for _ in range(TOP_K):
            pltpu.make_async_copy(buf_ref.at[s], xs_hbm.at[pl.ds(0, td), :], sems.at[s]).wait()

    @pl.when(i >= 2)
    def _():
        wait_slot(slot)

    buf_ref[slot] = h_ref[...]

    def issue(r, c):
        for k in range(TOP_K):
            p = pos_ref[(i * td + r) * TOP_K + k]
            pltpu.make_async_copy(buf_ref.at[slot, pl.ds(r, 1), :], xs_hbm.at[pl.ds(p, 1), :],
                                  sems.at[slot]).start()
        return c
    lax.fori_loop(0, td, issue, 0, unroll=8)

    @pl.when(i == n_steps - 1)
    def _():
        wait_slot(slot)
        if n_steps > 1:
            wait_slot(1 - slot)


def _moe_dispatch(pos_flat, zero_tiles, h, *, n_rows, td, tm_rows):
    t, d = h.shape
    n_steps = t // td
    grid_spec = pltpu.PrefetchScalarGridSpec(
        num_scalar_prefetch=2,
        grid=(n_steps,),
        in_specs=[pl.BlockSpec((td, d), lambda i, pos, zt: (i, 0))],
        out_specs=pl.BlockSpec(memory_space=pl.ANY),
        scratch_shapes=[pltpu.VMEM((2, td, d), F32), pltpu.VMEM((td, d), F32),
                        pltpu.SemaphoreType.DMA((2,)), pltpu.SemaphoreType.DMA(())],
    )
    return pl.pallas_call(
        functools.partial(_dispatch_kernel, n_steps=n_steps, tm_rows=tm_rows),
        grid_spec=grid_spec,
        out_shape=jax.ShapeDtypeStruct((n_rows, d), F32),
        compiler_params=_cparams(1),
        name="moe_dispatch",
    )(pos_flat, zero_tiles, h)


def _moe_kernel(tile_expert_ref, n_active_ref, xs_ref, wg_ref, wu_ref, wd_ref, o_ref, xb_ref, acc_ref):
    i = pl.program_id(0)
    j = pl.program_id(1)
    nj = pl.num_programs(1)
    active = i < n_active_ref[0]

    @pl.when(jnp.logical_and(active, j == 0))
    def _():
        xb_ref[...] = xs_ref[...].astype(BF16)

    @pl.when(active)
    def _():
        xb = xb_ref[...]
        g = jnp.dot(xb, wg_ref[0], preferred_element_type=F32)
        u = jnp.dot(xb, wu_ref[0], preferred_element_type=F32)
        act = (g * jax.nn.sigmoid(g) * u).astype(BF16)
        part = jnp.dot(act, wd_ref[0], preferred_element_type=F32)

        @pl.when(j == 0)
        def _():
            acc_ref[...] = part

        @pl.when(j > 0)
        def _():
            acc_ref[...] += part

    @pl.when(j == nj - 1)
    def _():
        o_ref[...] = jnp.where(active, acc_ref[...], 0.0)


def _moe_experts(tile_expert, n_active, xs, wg, wu, wd, *, tm, tf):
    n_rows, d = xs.shape
    d_ff = wg.shape[2]
    grid_spec = pltpu.PrefetchScalarGridSpec(
        num_scalar_prefetch=2,
        grid=(n_rows // tm, d_ff // tf),
        in_specs=[
            pl.BlockSpec((tm, d), lambda i, j, te, na: (jnp.minimum(i, na[0] - 1), 0)),
            pl.BlockSpec((1, d, tf), lambda i, j, te, na: (te[i], 0, j)),
            pl.BlockSpec((1, d, tf), lambda i, j, te, na: (te[i], 0, j)),
            pl.BlockSpec((1, tf, d), lambda i, j, te, na: (te[i], j, 0)),
        ],
        out_specs=pl.BlockSpec((tm, d), lambda i, j, te, na: (i, 0)),
        scratch_shapes=[pltpu.VMEM((tm, d), BF16), pltpu.VMEM((tm, d), F32)],
    )
    return pl.pallas_call(
        _moe_kernel,
        grid_spec=grid_spec,
        out_shape=jax.ShapeDtypeStruct((n_rows, d), F32),
        compiler_params=_cparams(2),
        name="moe_experts",
    )(tile_expert, n_active, xs, wg, wu, wd)


def _combine_kernel(pos_ref, y_hbm, route_ref, h_ref, lng_ref, lnb_ref, o_ref, ybuf_ref, sems, *, alpha, n_steps):
    i = pl.program_id(0)
    tm = h_ref.shape[0]
    n_rows = TOP_K * tm
    slot = i % 2

    def issue(step, s):
        def body(r, c):
            p = pos_ref[step * n_rows + r]
            pltpu.make_async_copy(y_hbm.at[pl.ds(p, 1), :], ybuf_ref.at[s, pl.ds(r, 1), :], sems.at[s]).start()
            return c
        lax.fori_loop(0, n_rows, body, 0, unroll=16)

    @pl.when(i == 0)
    def _():
        issue(0, 0)

    @pl.when(i + 1 < n_steps)
    def _():
        issue(i + 1, 1 - slot)

    pltpu.make_async_copy(y_hbm.at[pl.ds(0, n_rows), :], ybuf_ref.at[slot], sems.at[slot]).wait()
    f = None
    for k in range(TOP_K):
        term = route_ref[:, TOP_K + k:TOP_K + k + 1] * ybuf_ref[slot, k * tm:(k + 1) * tm, :]
        f = term if f is None else f + term
    o_ref[...] = _layernorm(alpha * h_ref[...] + f, lng_ref[...], lnb_ref[...])


def _moe_combine(pos_tiles, y_sorted, route, h, ln_g, ln_b, *, alpha, tm):
    t, d = h.shape
    n_steps = t // tm
    grid_spec = pltpu.PrefetchScalarGridSpec(
        num_scalar_prefetch=1,
        grid=(n_steps,),
        in_specs=[
            pl.BlockSpec(memory_space=pl.ANY),
            pl.BlockSpec((tm, LANES), lambda i, pos: (i, 0)),
            pl.BlockSpec((tm, d), lambda i, pos: (i, 0)),
            pl.BlockSpec((1, d), lambda i, pos: (0, 0)),
            pl.BlockSpec((1, d), lambda i, pos: (0, 0)),
        ],
        out_specs=pl.BlockSpec((tm, d), lambda i, pos: (i, 0)),
        scratch_shapes=[pltpu.VMEM((2, TOP_K * tm, d), F32), pltpu.SemaphoreType.DMA((2,))],
    )
    return pl.pallas_call(
        functools.partial(_combine_kernel, alpha=alpha, n_steps=n_steps),
        grid_spec=grid_spec,
        out_shape=jax.ShapeDtypeStruct((t, d), F32),
        compiler_params=_cparams(1),
        name="moe_combine",
    )(pos_tiles, y_sorted, route, h, ln_g, ln_b)


def _route_plan(route, counts, n_experts, tm_rows, tm_comb):
    t = route.shape[0]
    experts = route[:, 0:TOP_K].astype(jnp.int32)
    ranks = route[:, 2 * TOP_K:3 * TOP_K].astype(jnp.int32)
    counts = counts[0, :n_experts].astype(jnp.int32)
    padded = ((counts + tm_rows - 1) // tm_rows) * tm_rows
    ends = jnp.cumsum(padded)
    starts = ends - padded
    onehot = (experts[:, :, None] == jnp.arange(n_experts, dtype=jnp.int32)).astype(jnp.int32)
    pos = jnp.sum(onehot * starts, axis=-1) + ranks
    n_rows = t * TOP_K + n_experts * tm_rows
    n_tiles = n_rows // tm_rows
    n_active = (ends[-1] // tm_rows).astype(jnp.int32)
    tile_start = jnp.arange(n_tiles, dtype=jnp.int32) * tm_rows
    tile_expert = jnp.sum((tile_start[:, None] >= ends[None, :]).astype(jnp.int32), axis=1)
    last_expert = jnp.sum((ends[-1] - 1 >= ends).astype(jnp.int32))
    tile_expert = jnp.minimum(tile_expert, last_expert).astype(jnp.int32)
    last_tile = jnp.where(padded > 0, ends // tm_rows - 1, -1)
    tail_tile = n_active + jnp.arange(n_experts, dtype=jnp.int32)
    tail_tile = jnp.where(tail_tile < n_tiles, tail_tile, -1)
    zero_tiles = jnp.concatenate([last_tile, tail_tile]).astype(jnp.int32)
    pos_tiles = pos.reshape(t // tm_comb, tm_comb, TOP_K).transpose(0, 2, 1).reshape(-1)
    return tile_expert, n_active.reshape(1), zero_tiles, pos.reshape(-1), pos_tiles, n_rows


def _pick(n, pref):
    t = min(n, pref)
    assert n % t == 0, (n, pref)
    return t


def kernel(x, ln_in_g, ln_in_b, w_in, b_in, lam_q1, lam_k1, lam_q2, lam_k2, subln_g, conv_w, conv_b, rg_wa, rg_ba, rg_wx, rg_bx, rg_lam, w_pa, w_pb, w_o, b_o, ln1_g, ln1_b, ffn_wg, ffn_wu, ffn_wd, moe_wr, moe_br, moe_wg, moe_wu, moe_wd, ln2_g, ln2_b):
    batch, seq, d = x.shape
    depth = w_in.shape[0]
    t = batch * seq
    attn_w = w_pa.shape[1]
    lru_w = w_pb.shape[1]
    qk_w = (w_in.shape[2] - attn_w - 2 * lru_w - 2 * d) // 2
    n_experts = moe_wr.shape[-1]
    alpha = (2.0 * depth) ** 0.25

    tm_proj = _pick(seq, 512)
    tq = _pick(seq, 2048)
    tm_merge = _pick(t, 512)
    tm_ffn = _pick(t, 512)
    tm_moe = _pick(t, 512)
    tm_comb = _pick(t, 512)
    cw = MXU_DIM

    half = HEAD_DIM // 2
    inv = 1.0 / (ROPE_THETA ** (jnp.arange(half, dtype=F32) * (2.0 / HEAD_DIM)))
    ang = jnp.arange(seq, dtype=jnp.int32).astype(F32)[:, None] * inv[None, :]
    cos_t = jnp.tile(jnp.cos(ang), (1, LANES // half))
    sin_t = jnp.tile(jnp.concatenate([-jnp.sin(ang), jnp.sin(ang)], axis=-1), (1, LANES // HEAD_DIM))

    row2 = lambda v: v.reshape(1, -1)
    h = x.reshape(t, d)
    w_in_next = None
    for l in range(depth):
        lambda_init = 0.8 - 0.6 * math.exp(-0.3 * l)
        lam = (jnp.exp(jnp.sum(lam_q1[l] * lam_k1[l])) - jnp.exp(jnp.sum(lam_q2[l] * lam_k2[l]))
               + lambda_init).reshape(1).astype(F32)
        w_in_l = w_in[l].astype(BF16) if w_in_next is None else w_in_next[0]
        outs = _inproj(h, row2(ln_in_g), row2(ln_in_b), w_in_l, row2(b_in[l]), cos_t, sin_t,
                       apply_ln=(l == 0), batch=batch, seq=seq, qk_w=qk_w, v_w=attn_w, lru_w=lru_w, tm=tm_proj)
        if l == 0:
            h, *outs = outs
        qt, kk, vt, xr, gr_act, ga, gb = outs

        is_moe = l % 2 == 1
        j = l // 2
        side = [moe_wg[j], moe_wu[j], moe_wd[j]] if is_moe else [(ffn_wg, j), (ffn_wu, j)]
        if l + 1 < depth:
            side.append((w_in, l + 1))
        att, side = _attention(lam, qt, kk, vt, row2(subln_g[l]), out_scale=1.0 - lambda_init, tq=tq,
                               to_bf16=tuple(side))
        w_in_next = side[-1] if l + 1 < depth else None
        expert_w = side[:3] if is_moe else ()

        nb = cw // LRU_BLOCK
        def dense_tiles(w):
            w = w.reshape(2, lru_w // cw, nb, LRU_BLOCK, LRU_BLOCK)
            eye = jnp.eye(nb, dtype=w.dtype)
            return jnp.einsum('dtncm,nk->dtnckm', w, eye).reshape(2, lru_w // cw, cw, cw)
        wbd = jnp.stack([dense_tiles(rg_wa[l]), dense_tiles(rg_wx[l])], axis=1).astype(BF16)
        hb = _lru(xr, gr_act, conv_w[l], row2(conv_b[l]), wbd, rg_ba[l], rg_bx[l], rg_lam[l],
                  batch=batch, seq=seq, cw=cw)

        is_moe = l % 2 == 1
        router = None
        if is_moe:
            j = l // 2
            wr = jnp.zeros((d, LANES), F32).at[:, :n_experts].set(moe_wr[j])
            br = jnp.zeros((1, LANES), F32).at[0, :n_experts].set(moe_br[j])
            wr_hi = wr.astype(BF16)
            wr_lo = (wr - wr_hi.astype(F32)).astype(BF16)
            router = (wr_hi, wr_lo, br, n_experts)
        res = _merge(att, hb, ga, gb, h, w_pa[l].astype(BF16), w_pb[l].astype(BF16), w_o[l].astype(BF16),
                     row2(b_o[l]), row2(ln1_g[l]), row2(ln1_b[l]), router, alpha=alpha, tm=tm_merge)
        if not is_moe:
            h = res
            j = l // 2
            d_ff = ffn_wg.shape[2]
            f_chunk = d_ff // 2 if (d_ff // 2) % LANES == 0 else d_ff
            h = _ffn(h, side[0][0], side[1][0], ffn_wd[j].astype(BF16),
                     row2(ln2_g[l]), row2(ln2_b[l]), alpha=alpha, tm=tm_ffn, f_chunk=f_chunk)
        else:
            h, route, counts = res
            d_ff = moe_wg.shape[3]
            tf = d_ff // 2 if (d_ff // 2) % LANES == 0 else d_ff
            tile_expert, n_active, zero_tiles, pos_flat, pos_tiles, n_rows = _route_plan(
                route, counts, n_experts, tm_moe, tm_comb)
            xs = _moe_dispatch(pos_flat, zero_tiles, h, n_rows=n_rows, td=tm_comb, tm_rows=tm_moe)
            y_sorted = _moe_experts(tile_expert, n_active, xs,
                                    *expert_w,
                                    tm=tm_moe, tf=tf)
            h = _moe_combine(pos_tiles, y_sorted, route, h, row2(ln2_g[l]), row2(ln2_b[l]),
                             alpha=alpha, tm=tm_comb)
    return h.reshape(batch, seq, d)
```

```python
import functools
import math

import jax
import jax.numpy as jnp
from jax import lax
from jax.experimental import pallas as pl
from jax.experimental.pallas import tpu as pltpu

F32 = jnp.float32
BF16 = jnp.bfloat16

HEAD_DIM = 64
V_HEAD_DIM = 2 * HEAD_DIM
LANES = 128
SUBLANES = 8
MXU_DIM = 256
ROPE_THETA = 10000.0
LRU_BLOCK = 64
LRU_C = 8.0
CONV_WIDTH = 4
TOP_K = 2
LN_EPS = 1e-5
VMEM_LIMIT = 52 * 1024 * 1024


def _cparams(n_axes, vmem=VMEM_LIMIT):
    return pltpu.CompilerParams(dimension_semantics=("arbitrary",) * n_axes, vmem_limit_bytes=vmem)


def _layernorm(x, g, b):
    mu = jnp.mean(x, axis=-1, keepdims=True)
    xc = x - mu
    var = jnp.mean(xc * xc, axis=-1, keepdims=True)
    return xc * lax.rsqrt(var + LN_EPS) * g + b


def _gelu_tanh(x):
    return 0.5 * x * (1.0 + jnp.tanh(math.sqrt(2.0 / math.pi) * (x + 0.044715 * (x * x * x))))


def _resident(shape):
    nd = len(shape)
    return pl.BlockSpec(shape, lambda *_: (0,) * nd, pipeline_mode=pl.Buffered(1))


def _inproj_kernel(x_ref, lng_ref, lnb_ref, w_ref, b_ref, cos_ref, sin_ref, *outs,
                   apply_ln, qk_w, v_w, lru_w, d_model):
    if apply_ln:
        h_ref, qt_ref, k_ref, vt_ref, xr_ref, gr_ref, ga_ref, gb_ref = outs
    else:
        qt_ref, k_ref, vt_ref, xr_ref, gr_ref, ga_ref, gb_ref = outs
    x = x_ref[...]
    if apply_ln:
        x = _layernorm(x, lng_ref[...], lnb_ref[...])
        h_ref[...] = x
    hb = x.astype(BF16)
    tm = x.shape[0]

    def proj(lo, width):
        return jnp.dot(hb, w_ref[:, lo:lo + width], preferred_element_type=F32) + b_ref[:, lo:lo + width]

    cos = cos_ref[...]
    sin = sin_ref[...]
    lane = lax.broadcasted_iota(jnp.int32, (tm, LANES), 1)
    first_half = (lane % HEAD_DIM) < (HEAD_DIM // 2)

    def rope(t):
        swapped = jnp.where(first_half,
                            pltpu.roll(t, LANES - HEAD_DIM // 2, axis=1),
                            pltpu.roll(t, HEAD_DIM // 2, axis=1))
        return t * cos + swapped * sin

    off = 0
    q = proj(off, qk_w)
    off += qk_w
    scale = HEAD_DIM ** -0.5 * math.log2(math.e)
    for c in range(qk_w // LANES):
        r = rope(q[:, c * LANES:(c + 1) * LANES]) * scale
        qt_ref[0, c * LANES:(c + 1) * LANES, :] = r.T.astype(BF16)
    k = proj(off, qk_w)
    off += qk_w
    for c in range(qk_w // LANES):
        k_ref[:, c * LANES:(c + 1) * LANES] = rope(k[:, c * LANES:(c + 1) * LANES]).astype(BF16)
    v = proj(off, v_w)
    off += v_w
    for c in range(v_w // LANES):
        vt_ref[0, c * LANES:(c + 1) * LANES, :] = v[:, c * LANES:(c + 1) * LANES].T.astype(BF16)
    xr_ref[...] = proj(off, lru_w)
    off += lru_w
    gr_ref[...] = _gelu_tanh(proj(off, lru_w)).astype(BF16)
    off += lru_w
    ga_ref[...] = jax.nn.sigmoid(proj(off, d_model)).astype(BF16)
    off += d_model
    gb_ref[...] = jax.nn.sigmoid(proj(off, d_model)).astype(BF16)


def _inproj(x2d, ln_g, ln_b, w, b, cos_t, sin_t, *, apply_ln, batch, seq, qk_w, v_w, lru_w, tm):
    t, d = x2d.shape
    n_cols = w.shape[1]
    spt = seq // tm
    row = lambda i: (i, 0)
    tr = lambda i: (i // spt, 0, i % spt)
    out_shape = [
        jax.ShapeDtypeStruct((batch, qk_w, seq), BF16),
        jax.ShapeDtypeStruct((t, qk_w), BF16),
        jax.ShapeDtypeStruct((batch, v_w, seq), BF16),
        jax.ShapeDtypeStruct((t, lru_w), F32),
        jax.ShapeDtypeStruct((t, lru_w), BF16),
        jax.ShapeDtypeStruct((t, d), BF16),
        jax.ShapeDtypeStruct((t, d), BF16),
    ]
    out_specs = [
        pl.BlockSpec((1, qk_w, tm), tr),
        pl.BlockSpec((tm, qk_w), row),
        pl.BlockSpec((1, v_w, tm), tr),
        pl.BlockSpec((tm, lru_w), row),
        pl.BlockSpec((tm, lru_w), row),
        pl.BlockSpec((tm, d), row),
        pl.BlockSpec((tm, d), row),
    ]
    if apply_ln:
        out_shape = [jax.ShapeDtypeStruct((t, d), F32)] + out_shape
        out_specs = [pl.BlockSpec((tm, d), row)] + out_specs
    kern = functools.partial(_inproj_kernel, apply_ln=apply_ln, qk_w=qk_w, v_w=v_w, lru_w=lru_w, d_model=d)
    return pl.pallas_call(
        kern,
        grid=(t // tm,),
        in_specs=[
            pl.BlockSpec((tm, d), row),
            _resident((1, d)), _resident((1, d)),
            _resident((d, n_cols)), _resident((1, n_cols)),
            pl.BlockSpec((tm, LANES), lambda i: (i % spt, 0)),
            pl.BlockSpec((tm, LANES), lambda i: (i % spt, 0)),
        ],
        out_specs=out_specs,
        out_shape=out_shape,
        compiler_params=_cparams(1),
        name="inproj",
    )(x2d, ln_g, ln_b, w, b, cos_t, sin_t)


def _attn_kernel(lam_ref, qt_ref, k_ref, vt_ref, g_ref, *rest, out_scale, chunk, qw, lag, n_cast):
    cast_in = rest[:n_cast]
    o_ref = rest[n_cast]
    cast_out = rest[n_cast + 1:2 * n_cast + 1]
    vaug_ref, s1_ref, s2_ref = rest[2 * n_cast + 1:]
    i = pl.program_id(2)
    n_ones = vaug_ref.shape[0] - V_HEAD_DIM

    for src, dst in zip(cast_in, cast_out):
        dst[...] = src[...].astype(dst.dtype)

    @pl.when(i == 0)
    def _():
        vaug_ref[0:V_HEAD_DIM, :] = vt_ref[0]
        vaug_ref[V_HEAD_DIM:, :] = jnp.ones((n_ones, vaug_ref.shape[1]), BF16)

    seq = k_ref.shape[0]
    tq = qt_ref.shape[2]
    n_chunks = seq // chunk
    row = lax.broadcasted_iota(jnp.int32, (V_HEAD_DIM, qw), 0)
    units = [(qb, mp) for qb in range(tq // qw) for mp in range(2)]
    bufs = (s1_ref, s2_ref)

    def q_unit(u):
        qb, mp = units[u]
        q = qt_ref[0, :, qb * qw:(qb + 1) * qw]
        keep = (row < HEAD_DIM) if mp == 0 else (row >= HEAD_DIM)
        return jnp.where(keep, q, jnp.zeros_like(q))

    shift = n_chunks + lag
    assert shift < 2 * n_chunks
    n_tasks = len(units) * n_chunks
    q_cur, m_run, m_chunk, acc, outs = None, None, {}, None, []
    for t in range(n_tasks + shift):
        if t < n_tasks:
            u, c = divmod(t, n_chunks)
            if c == 0:
                q_cur, m_run = q_unit(u), None
            s = jnp.dot(k_ref[c * chunk:(c + 1) * chunk, :], q_cur, preferred_element_type=F32)
            mc = jnp.max(s, axis=0, keepdims=True)
            m_run = mc if m_run is None else jnp.maximum(m_run, mc)
            m_chunk[u, c] = m_run
            bufs[u % 2][c * chunk:(c + 1) * chunk, :] = jnp.exp2((s - m_run).astype(BF16))
        if t >= shift:
            u, c = divmod(t - shift, n_chunks)
            e = bufs[u % 2][c * chunk:(c + 1) * chunk, :]
            part = jnp.dot(vaug_ref[:, c * chunk:(c + 1) * chunk], e, preferred_element_type=F32)
            part = part * jnp.exp2(m_chunk[u, c] - m_chunk[u, n_chunks - 1])
            acc = part if c == 0 else acc + part
            if c == n_chunks - 1:
                outs.append(acc[0:V_HEAD_DIM, :] * (1.0 / acc[V_HEAD_DIM:V_HEAD_DIM + 1, :]))

    for qb in range(tq // qw):
        o = (outs[2 * qb] - lam_ref[0] * outs[2 * qb + 1]).T
        ms = jnp.mean(o * o, axis=-1, keepdims=True)
        o_ref[qb * qw:(qb + 1) * qw, :] = (o * lax.rsqrt(ms + LN_EPS) * g_ref[...] * out_scale).astype(o_ref.dtype)


def _attention(lam, qt, k, vt, subln_g, *, out_scale, tq, to_bf16=()):
    batch, qk_w, seq = qt.shape
    heads = qk_w // V_HEAD_DIM
    nq = seq // tq
    n_steps = batch * heads * nq
    n_ones = 16
    chunk = _pick(seq, 256)
    kern = functools.partial(_attn_kernel, out_scale=out_scale, chunk=chunk, qw=MXU_DIM,
                             lag=min(4, seq // chunk - 1), n_cast=len(to_bf16))
    cast_in_specs, cast_out_specs, cast_shapes, cast_args = [], [], [], []
    for entry in to_bf16:
        w, e0 = entry if isinstance(entry, tuple) else (entry, None)
        n_e, n_r, n_c = (1 if e0 is not None else w.shape[0]), w.shape[1], w.shape[2]
        per = n_e * n_r // n_steps
        assert per * n_steps == n_e * n_r and n_r % per == 0 and per % 16 == 0, (w.shape, n_steps)
        per_e = n_r // per

        def index(b, h, i, per_e=per_e, e0=0):
            s = (b * heads + h) * nq + i
            return (s // per_e + e0, s % per_e, 0)
        cast_in_specs.append(pl.BlockSpec((1, per, n_c), functools.partial(index, e0=e0 or 0)))
        cast_out_specs.append(pl.BlockSpec((1, per, n_c), index))
        cast_shapes.append(jax.ShapeDtypeStruct((n_e, n_r, n_c), BF16))
        cast_args.append(w)
    res = pl.pallas_call(
        kern,
        grid=(batch, heads, nq),
        in_specs=[
            pl.BlockSpec(memory_space=pltpu.SMEM),
            pl.BlockSpec((1, V_HEAD_DIM, tq), lambda b, h, i: (b, h, i)),
            pl.BlockSpec((seq, V_HEAD_DIM), lambda b, h, i: (b, h)),
            pl.BlockSpec((1, V_HEAD_DIM, seq), lambda b, h, i: (b, h, 0)),
            pl.BlockSpec((1, V_HEAD_DIM), lambda b, h, i: (0, 0)),
        ] + cast_in_specs,
        out_specs=[pl.BlockSpec((tq, V_HEAD_DIM), lambda b, h, i: (b * nq + i, h))] + cast_out_specs,
        out_shape=[jax.ShapeDtypeStruct((batch * seq, qk_w), BF16)] + cast_shapes,
        scratch_shapes=[pltpu.VMEM((V_HEAD_DIM + n_ones, seq), BF16),
                        pltpu.VMEM((seq, MXU_DIM), BF16), pltpu.VMEM((seq, MXU_DIM), BF16)],
        compiler_params=_cparams(3),
        name="diff_attn",
    )(lam, qt, k, vt, subln_g, *cast_args)
    return res[0], tuple(res[1:])


def _lru_kernel(xr_ref, gr_ref, cw_ref, cb_ref, wbd_ref, ba_ref, bx_ref, lam_ref, o_ref,
                xpad_ref, xc_ref, a_ref, b_ref, h_ref, *, chunk):
    seq, cw = xr_ref.shape
    pad = SUBLANES
    zeros = jnp.zeros((pad, cw), F32)
    xpad_ref[0:pad, :] = zeros
    xpad_ref[pad + seq:, :] = zeros
    xpad_ref[pad:pad + seq, :] = xr_ref[...]
    n_chunks = seq // chunk

    for c in range(n_chunks):
        lo = c * chunk
        acc = cb_ref[...] + cw_ref[0:1, :] * xpad_ref[pad + lo - 2:pad + lo - 2 + chunk, :]
        for tap in range(1, CONV_WIDTH):
            s0 = pad + lo - 2 + tap
            acc = acc + cw_ref[tap:tap + 1, :] * xpad_ref[s0:s0 + chunk, :]
        xc_ref[lo:lo + chunk, :] = acc

    row = lax.broadcasted_iota(jnp.int32, (SUBLANES, cw), 0)
    n_blocks = seq // SUBLANES

    for d in range(2):
        neg_lam = -lam_ref[d:d + 1, :]
        softplus = jnp.maximum(neg_lam, 0.0) + jnp.log1p(jnp.exp(-jnp.abs(neg_lam)))
        for c in range(n_chunks):
            lo = c * chunk
            xc = xc_ref[lo:lo + chunk, :]
            xb = xc.astype(BF16)
            zr = jnp.dot(xb, wbd_ref[d, 0, 0], preferred_element_type=F32) + ba_ref[d:d + 1, :]
            zg = jnp.dot(xb, wbd_ref[d, 1, 0], preferred_element_type=F32) + bx_ref[d:d + 1, :]
            r = 0.5 + 0.5 * jnp.tanh(0.5 * zr)
            g = 0.5 + 0.5 * jnp.tanh(0.5 * zg)
            log_a = -LRU_C * r * softplus
            a_ref[lo:lo + chunk, :] = jnp.exp(log_a)
            th = jnp.tanh(log_a)
            u = -2.0 * th
            root = jnp.where(u > 0.0, u * lax.rsqrt(u * (1.0 - th)), 0.0)
            b_ref[lo:lo + chunk, :] = root * (g * xc)

        reverse = d == 1

        def body(j, carry):
            blk = (n_blocks - 1 - j) if reverse else j
            r0 = pl.multiple_of(blk * SUBLANES, SUBLANES)
            a = a_ref[pl.ds(r0, SUBLANES), :]
            b = b_ref[pl.ds(r0, SUBLANES), :]
            for step in (1, 2, 4):
                if reverse:
                    shift, valid = SUBLANES - step, row < SUBLANES - step
                else:
                    shift, valid = step, row >= step
                a_sh = pltpu.roll(a, shift, axis=0)
                b_sh = pltpu.roll(b, shift, axis=0)
                b = jnp.where(valid, a * b_sh + b, b)
                a = jnp.where(valid, a * a_sh, a)
            h = b + a * carry
            if reverse:
                h_ref[pl.ds(r0, SUBLANES), :] = h_ref[pl.ds(r0, SUBLANES), :] + h
                return h[0:1, :]
            h_ref[pl.ds(r0, SUBLANES), :] = h
            return h[SUBLANES - 1:SUBLANES, :]

        lax.fori_loop(0, n_blocks, body, jnp.zeros((1, cw), F32), unroll=4)

    o_ref[...] = (h_ref[...] * gr_ref[...].astype(F32)).astype(o_ref.dtype)


def _lru(xr, gr_act, conv_w, conv_b, wbd, ba, bx, lam, *, batch, seq, cw):
    t, c = xr.shape
    blk = lambda b, j: (b, j)
    par = lambda b, j: (0, j)
    return pl.pallas_call(
        functools.partial(_lru_kernel, chunk=min(512, seq)),
        grid=(batch, c // cw),
        in_specs=[
            pl.BlockSpec((seq, cw), blk),
            pl.BlockSpec((seq, cw), blk),
            pl.BlockSpec((CONV_WIDTH, cw), par),
            pl.BlockSpec((1, cw), par),
            pl.BlockSpec((2, 2, 1, cw, cw), lambda b, j: (0, 0, j, 0, 0)),
            pl.BlockSpec((2, cw), par),
            pl.BlockSpec((2, cw), par),
            pl.BlockSpec((2, cw), par),
        ],
        out_specs=pl.BlockSpec((seq, cw), blk),
        out_shape=jax.ShapeDtypeStruct((t, c), BF16),
        scratch_shapes=[
            pltpu.VMEM((seq + 2 * SUBLANES, cw), F32),
            pltpu.VMEM((seq, cw), F32),
            pltpu.VMEM((seq, cw), F32),
            pltpu.VMEM((seq, cw), F32),
            pltpu.VMEM((seq, cw), F32),
        ],
        compiler_params=_cparams(2),
        name="conv_rglru",
    )(xr, gr_act, conv_w, conv_b, wbd, ba, bx, lam)


def _merge_kernel(att_ref, hb_ref, ga_ref, gb_ref, h_ref, wpa_ref, wpb_ref, wo_ref, bo_ref,
                  lng_ref, lnb_ref, *rest, alpha, n_experts):
    if n_experts:
        wrh_ref, wrl_ref, br_ref, o_ref, route_ref, counts_ref, cnt_ref = rest
    else:
        (o_ref,) = rest
    branch_a = jnp.dot(att_ref[...], wpa_ref[...], preferred_element_type=F32)
    branch_b = jnp.dot(hb_ref[...], wpb_ref[...], preferred_element_type=F32)
    merged = ga_ref[...].astype(F32) * branch_a + gb_ref[...].astype(F32) * branch_b
    m = jnp.dot(merged.astype(BF16), wo_ref[...], preferred_element_type=F32) + bo_ref[...]
    y = _layernorm(alpha * h_ref[...] + m, lng_ref[...], lnb_ref[...])
    o_ref[...] = y
    if n_experts:
        y_hi = y.astype(BF16)
        y_lo = (y - y_hi.astype(F32)).astype(BF16)
        logits = (jnp.dot(y_hi, wrh_ref[...], preferred_element_type=F32)
                  + jnp.dot(y_lo, wrh_ref[...], preferred_element_type=F32)
                  + jnp.dot(y_hi, wrl_ref[...], preferred_element_type=F32)) + br_ref[...]
        lane = lax.broadcasted_iota(jnp.int32, logits.shape, 1)
        neg = jnp.float32(-jnp.inf)
        lg = jnp.where(lane < n_experts, logits, neg)
        v1 = jnp.max(lg, axis=-1, keepdims=True)
        i1 = jnp.min(jnp.where(lg == v1, lane, LANES), axis=-1, keepdims=True)
        lg2 = jnp.where(lane == i1, neg, lg)
        v2 = jnp.max(lg2, axis=-1, keepdims=True)
        i2 = jnp.min(jnp.where(lg2 == v2, lane, LANES), axis=-1, keepdims=True)
        e21 = jnp.exp(v2 - v1)
        g1 = 1.0 / (1.0 + e21)
        g2 = e21 * g1
        @pl.when(pl.program_id(0) == 0)
        def _():
            cnt_ref[...] = jnp.zeros_like(cnt_ref)

        tm = y.shape[0]
        oh1 = lane == i1
        oh2 = lane == i2
        assign = jnp.where(jnp.logical_or(oh1, oh2), 1.0, 0.0)
        tri = (lax.broadcasted_iota(jnp.int32, (tm, tm), 1)
               < lax.broadcasted_iota(jnp.int32, (tm, tm), 0)).astype(BF16)
        rank = jnp.dot(tri, assign.astype(BF16), preferred_element_type=F32) + cnt_ref[...]
        r1 = jnp.sum(jnp.where(oh1, rank, 0.0), axis=-1, keepdims=True)
        r2 = jnp.sum(jnp.where(oh2, rank, 0.0), axis=-1, keepdims=True)
        cnt_ref[...] += jnp.sum(assign, axis=0, keepdims=True)
        counts_ref[...] = cnt_ref[...]
        cols = (i1.astype(F32), i2.astype(F32), g1, g2, r1, r2)
        route = jnp.zeros_like(logits)
        for n, col in enumerate(cols):
            route = jnp.where(lane == n, col, route)
        route_ref[...] = route


def _merge(att, hb, ga, gb, h, wpa, wpb, wo, bo, ln_g, ln_b, router, *, alpha, tm):
    t, d = h.shape
    aw = att.shape[1]
    lw = hb.shape[1]
    row = lambda i: (i, 0)
    in_specs = [
        pl.BlockSpec((tm, aw), row), pl.BlockSpec((tm, lw), row),
        pl.BlockSpec((tm, d), row), pl.BlockSpec((tm, d), row), pl.BlockSpec((tm, d), row),
        _resident((aw, d)), _resident((lw, d)), _resident((d, d)), _resident((1, d)),
        _resident((1, d)), _resident((1, d)),
    ]
    args = [att, hb, ga, gb, h, wpa, wpb, wo, bo, ln_g, ln_b]
    out_shape = [jax.ShapeDtypeStruct((t, d), F32)]
    out_specs = [pl.BlockSpec((tm, d), row)]
    n_experts = 0
    scratch = []
    if router is not None:
        wr_hi, wr_lo, br, n_experts = router
        in_specs += [_resident((d, LANES)), _resident((d, LANES)), _resident((1, LANES))]
        args += [wr_hi, wr_lo, br]
        out_shape += [jax.ShapeDtypeStruct((t, LANES), F32), jax.ShapeDtypeStruct((1, LANES), F32)]
        out_specs += [pl.BlockSpec((tm, LANES), row), pl.BlockSpec((1, LANES), lambda i: (0, 0))]
        scratch = [pltpu.VMEM((1, LANES), F32)]
    res = pl.pallas_call(
        functools.partial(_merge_kernel, alpha=alpha, n_experts=n_experts),
        grid=(t // tm,),
        in_specs=in_specs,
        out_specs=out_specs,
        out_shape=out_shape,
        scratch_shapes=scratch,
        compiler_params=_cparams(1),
        name="merge_outproj",
    )(*args)
    return res if router is not None else res[0]


def _ffn_kernel(h_ref, wg_ref, wu_ref, wd_ref, lng_ref, lnb_ref, o_ref, *, alpha, f_chunk):
    h = h_ref[...]
    hb = h.astype(BF16)
    d_ff = wg_ref.shape[1]
    f = None
    for lo in range(0, d_ff, f_chunk):
        g = jnp.dot(hb, wg_ref[:, lo:lo + f_chunk], preferred_element_type=F32)
        u = jnp.dot(hb, wu_ref[:, lo:lo + f_chunk], preferred_element_type=F32)
        act = (g * jax.nn.sigmoid(g) * u).astype(BF16)
        part = jnp.dot(act, wd_ref[lo:lo + f_chunk, :], preferred_element_type=F32)
        f = part if f is None else f + part
    o_ref[...] = _layernorm(alpha * h + f, lng_ref[...], lnb_ref[...])


def _ffn(h, wg, wu, wd, ln_g, ln_b, *, alpha, tm, f_chunk):
    t, d = h.shape
    d_ff = wg.shape[1]
    row = lambda i: (i, 0)
    return pl.pallas_call(
        functools.partial(_ffn_kernel, alpha=alpha, f_chunk=f_chunk),
        grid=(t // tm,),
        in_specs=[pl.BlockSpec((tm, d), row), _resident((d, d_ff)), _resident((d, d_ff)),
                  _resident((d_ff, d)), _resident((1, d)), _resident((1, d))],
        out_specs=pl.BlockSpec((tm, d), row),
        out_shape=jax.ShapeDtypeStruct((t, d), F32),
        compiler_params=_cparams(1),
        name="dense_swiglu",
    )(h, wg, wu, wd, ln_g, ln_b)


def _dispatch_kernel(pos_ref, ztile_ref, h_ref, xs_hbm, buf_ref, zero_ref, sems, zsem, *, n_steps, tm_rows):
    i = pl.program_id(0)
    td = h_ref.shape[0]
    slot = i % 2

    @pl.when(i == 0)
    def _():
        zero_ref[...] = jnp.zeros_like(zero_ref)
        parts = tm_rows // td

        def fill(n, c):
            @pl.when(ztile_ref[n] >= 0)
            def _():
                base = ztile_ref[n] * tm_rows
                for part in range(parts):
                    pltpu.make_async_copy(zero_ref, xs_hbm.at[pl.ds(base + part * td, td), :], zsem).start()
            return c
        lax.fori_loop(0, ztile_ref.shape[0], fill, 0)

        def drain(n, c):
            @pl.when(ztile_ref[n] >= 0)
            def _():
                for part in range(parts):
                    pltpu.make_async_copy(zero_ref, xs_hbm.at[pl.ds(0, td), :], zsem).wait()
            return c
        lax.fori_loop(0, ztile_ref.shape[0], drain, 0)

    def wait_slot(s):
        for _ in range(TOP_K):
            pltpu.make_async_copy(buf_ref.at[s], xs_hbm.at[pl.ds(0, td), :], sems.at[s]).wait()

    @pl.when(i >= 2)
    def _():
        wait_slot(slot)

    buf_ref[slot] = h_ref[...]

    def issue(r, c):
        for k in range(TOP_K):
            p = pos_ref[(i * td + r) * TOP_K + k]
            pltpu.make_async_copy(buf_ref.at[slot, pl.ds(r, 1), :], xs_hbm.at[pl.ds(p, 1), :],
                                  sems.at[slot]).start(priority=k % 2)
        return c
    lax.fori_loop(0, td, issue, 0, unroll=8)

    @pl.when(i == n_steps - 1)
    def _():
        wait_slot(slot)
        if n_steps > 1:
            wait_slot(1 - slot)


def _moe_dispatch(pos_flat, zero_tiles, h, *, n_rows, td, tm_rows):
    t, d = h.shape
    n_steps = t // td
    grid_spec = pltpu.PrefetchScalarGridSpec(
        num_scalar_prefetch=2,
        grid=(n_steps,),
        in_specs=[pl.BlockSpec((td, d), lambda i, pos, zt: (i, 0))],
        out_specs=pl.BlockSpec(memory_space=pl.ANY),
        scratch_shapes=[pltpu.VMEM((2, td, d), F32), pltpu.VMEM((td, d), F32),
                        pltpu.SemaphoreType.DMA((2,)), pltpu.SemaphoreType.DMA(())],
    )
    return pl.pallas_call(
        functools.partial(_dispatch_kernel, n_steps=n_steps, tm_rows=tm_rows),
        grid_spec=grid_spec,
        out_shape=jax.ShapeDtypeStruct((n_rows, d), F32),
        compiler_params=_cparams(1),
        name="moe_dispatch",
    )(pos_flat, zero_tiles, h)


def _moe_kernel(tile_expert_ref, n_active_ref, xs_ref, wg_ref, wu_ref, wd_ref, o_ref, xb_ref, acc_ref):
    i = pl.program_id(0)
    j = pl.program_id(1)
    nj = pl.num_programs(1)
    active = i < n_active_ref[0]

    @pl.when(jnp.logical_and(active, j == 0))
    def _():
        xb_ref[...] = xs_ref[...].astype(BF16)

    @pl.when(active)
    def _():
        xb = xb_ref[...]
        g = jnp.dot(xb, wg_ref[0], preferred_element_type=F32)
        u = jnp.dot(xb, wu_ref[0], preferred_element_type=F32)
        act = (g * jax.nn.sigmoid(g) * u).astype(BF16)
        part = jnp.dot(act, wd_ref[0], preferred_element_type=F32)

        @pl.when(j == 0)
        def _():
            acc_ref[...] = part

        @pl.when(j > 0)
        def _():
            acc_ref[...] += part

    @pl.when(j == nj - 1)
    def _():
        o_ref[...] = jnp.where(active, acc_ref[...], 0.0)


def _moe_experts(tile_expert, n_active, xs, wg, wu, wd, *, tm, tf):
    n_rows, d = xs.shape
    d_ff = wg.shape[2]
    grid_spec = pltpu.PrefetchScalarGridSpec(
        num_scalar_prefetch=2,
        grid=(n_rows // tm, d_ff // tf),
        in_specs=[
            pl.BlockSpec((tm, d), lambda i, j, te, na: (jnp.minimum(i, na[0] - 1), 0)),
            pl.BlockSpec((1, d, tf), lambda i, j, te, na: (te[i], 0, j)),
            pl.BlockSpec((1, d, tf), lambda i, j, te, na: (te[i], 0, j)),
            pl.BlockSpec((1, tf, d), lambda i, j, te, na: (te[i], j, 0)),
        ],
        out_specs=pl.BlockSpec((tm, d), lambda i, j, te, na: (i, 0)),
        scratch_shapes=[pltpu.VMEM((tm, d), BF16), pltpu.VMEM((tm, d), F32)],
    )
    return pl.pallas_call(
        _moe_kernel,
        grid_spec=grid_spec,
        out_shape=jax.ShapeDtypeStruct((n_rows, d), F32),
        compiler_params=_cparams(2),
        name="moe_experts",
    )(tile_expert, n_active, xs, wg, wu, wd)


def _combine_kernel(pos_ref, y_hbm, route_ref, h_ref, lng_ref, lnb_ref, o_ref, ybuf_ref, sems, *, alpha, n_steps):
    i = pl.program_id(0)
    tm = h_ref.shape[0]
    n_rows = TOP_K * tm
    slot = i % 2

    def issue(step, s):
        def body(r2, c):
            for q in range(2):
                r = 2 * r2 + q
                p = pos_ref[step * n_rows + r]
                pltpu.make_async_copy(y_hbm.at[pl.ds(p, 1), :], ybuf_ref.at[s, pl.ds(r, 1), :],
                                      sems.at[s]).start(priority=q)
            return c
        lax.fori_loop(0, n_rows // 2, body, 0, unroll=8)

    @pl.when(i == 0)
    def _():
        issue(0, 0)

    @pl.when(i + 1 < n_steps)
    def _():
        issue(i + 1, 1 - slot)

    pltpu.make_async_copy(y_hbm.at[pl.ds(0, n_rows), :], ybuf_ref.at[slot], sems.at[slot]).wait()
    f = None
    for k in range(TOP_K):
        term = route_ref[:, TOP_K + k:TOP_K + k + 1] * ybuf_ref[slot, k * tm:(k + 1) * tm, :]
        f = term if f is None else f + term
    o_ref[...] = _layernorm(alpha * h_ref[...] + f, lng_ref[...], lnb_ref[...])


def _moe_combine(pos_tiles, y_sorted, route, h, ln_g, ln_b, *, alpha, tm):
    t, d = h.shape
    n_steps = t // tm
    grid_spec = pltpu.PrefetchScalarGridSpec(
        num_scalar_prefetch=1,
        grid=(n_steps,),
        in_specs=[
            pl.BlockSpec(memory_space=pl.ANY),
            pl.BlockSpec((tm, LANES), lambda i, pos: (i, 0)),
            pl.BlockSpec((tm, d), lambda i, pos: (i, 0)),
            pl.BlockSpec((1, d), lambda i, pos: (0, 0)),
            pl.BlockSpec((1, d), lambda i, pos: (0, 0)),
        ],
        out_specs=pl.BlockSpec((tm, d), lambda i, pos: (i, 0)),
        scratch_shapes=[pltpu.VMEM((2, TOP_K * tm, d), F32), pltpu.SemaphoreType.DMA((2,))],
    )
    return pl.pallas_call(
        functools.partial(_combine_kernel, alpha=alpha, n_steps=n_steps),
        grid_spec=grid_spec,
        out_shape=jax.ShapeDtypeStruct((t, d), F32),
        compiler_params=_cparams(1),
        name="moe_combine",
    )(pos_tiles, y_sorted, route, h, ln_g, ln_b)


def _route_plan(route, counts, n_experts, tm_rows, tm_comb):
    t = route.shape[0]
    experts = route[:, 0:TOP_K].astype(jnp.int32)
    ranks = route[:, 2 * TOP_K:3 * TOP_K].astype(jnp.int32)
    counts = counts[0, :n_experts].astype(jnp.int32)
    padded = ((counts + tm_rows - 1) // tm_rows) * tm_rows
    ends = jnp.cumsum(padded)
    starts = ends - padded
    onehot = (experts[:, :, None] == jnp.arange(n_experts, dtype=jnp.int32)).astype(jnp.int32)
    pos = jnp.sum(onehot * starts, axis=-1) + ranks
    n_rows = t * TOP_K + n_experts * tm_rows
    n_tiles = n_rows // tm_rows
    n_active = (ends[-1] // tm_rows).astype(jnp.int32)
    tile_start = jnp.arange(n_tiles, dtype=jnp.int32) * tm_rows
    tile_expert = jnp.sum((tile_start[:, None] >= ends[None, :]).astype(jnp.int32), axis=1)
    last_expert = jnp.sum((ends[-1] - 1 >= ends).astype(jnp.int32))
    tile_expert = jnp.minimum(tile_expert, last_expert).astype(jnp.int32)
    last_tile = jnp.where(padded > 0, ends // tm_rows - 1, -1)
    tail_tile = n_active + jnp.arange(n_experts, dtype=jnp.int32)
    tail_tile = jnp.where(tail_tile < n_tiles, tail_tile, -1)
    zero_tiles = jnp.concatenate([last_tile, tail_tile]).astype(jnp.int32)
    pos_tiles = pos.reshape(t // tm_comb, tm_comb, TOP_K).transpose(0, 2, 1).reshape(-1)
    return tile_expert, n_active.reshape(1), zero_tiles, pos.reshape(-1), pos_tiles, n_rows


def _pick(n, pref):
    t = min(n, pref)
    assert n % t == 0, (n, pref)
    return t


def kernel(x, ln_in_g, ln_in_b, w_in, b_in, lam_q1, lam_k1, lam_q2, lam_k2, subln_g, conv_w, conv_b, rg_wa, rg_ba, rg_wx, rg_bx, rg_lam, w_pa, w_pb, w_o, b_o, ln1_g, ln1_b, ffn_wg, ffn_wu, ffn_wd, moe_wr, moe_br, moe_wg, moe_wu, moe_wd, ln2_g, ln2_b):
    batch, seq, d = x.shape
    depth = w_in.shape[0]
    t = batch * seq
    attn_w = w_pa.shape[1]
    lru_w = w_pb.shape[1]
    qk_w = (w_in.shape[2] - attn_w - 2 * lru_w - 2 * d) // 2
    n_experts = moe_wr.shape[-1]
    alpha = (2.0 * depth) ** 0.25

    tm_proj = _pick(seq, 512)
    tq = _pick(seq, 2048)
    tm_merge = _pick(t, 512)
    tm_ffn = _pick(t, 512)
    tm_moe = _pick(t, 512)
    tm_comb = _pick(t, 512)
    cw = MXU_DIM

    half = HEAD_DIM // 2
    inv = 1.0 / (ROPE_THETA ** (jnp.arange(half, dtype=F32) * (2.0 / HEAD_DIM)))
    ang = jnp.arange(seq, dtype=jnp.int32).astype(F32)[:, None] * inv[None, :]
    cos_t = jnp.tile(jnp.cos(ang), (1, LANES // half))
    sin_t = jnp.tile(jnp.concatenate([-jnp.sin(ang), jnp.sin(ang)], axis=-1), (1, LANES // HEAD_DIM))

    row2 = lambda v: v.reshape(1, -1)
    h = x.reshape(t, d)
    w_in_next = None
    for l in range(depth):
        lambda_init = 0.8 - 0.6 * math.exp(-0.3 * l)
        lam = (jnp.exp(jnp.sum(lam_q1[l] * lam_k1[l])) - jnp.exp(jnp.sum(lam_q2[l] * lam_k2[l]))
               + lambda_init).reshape(1).astype(F32)
        w_in_l = w_in[l].astype(BF16) if w_in_next is None else w_in_next[0]
        outs = _inproj(h, row2(ln_in_g), row2(ln_in_b), w_in_l, row2(b_in[l]), cos_t, sin_t,
                       apply_ln=(l == 0), batch=batch, seq=seq, qk_w=qk_w, v_w=attn_w, lru_w=lru_w, tm=tm_proj)
        if l == 0:
            h, *outs = outs
        qt, kk, vt, xr, gr_act, ga, gb = outs

        is_moe = l % 2 == 1
        j = l // 2
        side = [moe_wg[j], moe_wu[j], moe_wd[j]] if is_moe else [(ffn_wg, j), (ffn_wu, j)]
        if l + 1 < depth:
            side.append((w_in, l + 1))
        att, side = _attention(lam, qt, kk, vt, row2(subln_g[l]), out_scale=1.0 - lambda_init, tq=tq,
                               to_bf16=tuple(side))
        w_in_next = side[-1] if l + 1 < depth else None
        expert_w = side[:3] if is_moe else ()

        nb = cw // LRU_BLOCK
        def dense_tiles(w):
            w = w.reshape(2, lru_w // cw, nb, LRU_BLOCK, LRU_BLOCK)
            eye = jnp.eye(nb, dtype=w.dtype)
            return jnp.einsum('dtncm,nk->dtnckm', w, eye).reshape(2, lru_w // cw, cw, cw)
        wbd = jnp.stack([dense_tiles(rg_wa[l]), dense_tiles(rg_wx[l])], axis=1).astype(BF16)
        hb = _lru(xr, gr_act, conv_w[l], row2(conv_b[l]), wbd, rg_ba[l], rg_bx[l], rg_lam[l],
                  batch=batch, seq=seq, cw=cw)

        is_moe = l % 2 == 1
        router = None
        if is_moe:
            j = l // 2
            wr = jnp.zeros((d, LANES), F32).at[:, :n_experts].set(moe_wr[j])
            br = jnp.zeros((1, LANES), F32).at[0, :n_experts].set(moe_br[j])
            wr_hi = wr.astype(BF16)
            wr_lo = (wr - wr_hi.astype(F32)).astype(BF16)
            router = (wr_hi, wr_lo, br, n_experts)
        res = _merge(att, hb, ga, gb, h, w_pa[l].astype(BF16), w_pb[l].astype(BF16), w_o[l].astype(BF16),
                     row2(b_o[l]), row2(ln1_g[l]), row2(ln1_b[l]), router, alpha=alpha, tm=tm_merge)
        if not is_moe:
            h = res
            j = l // 2
            d_ff = ffn_wg.shape[2]
            f_chunk = d_ff // 2 if (d_ff // 2) % LANES == 0 else d_ff
            h = _ffn(h, side[0][0], side[1][0], ffn_wd[j].astype(BF16),
                     row2(ln2_g[l]), row2(ln2_b[l]), alpha=alpha, tm=tm_ffn, f_chunk=f_chunk)
        else:
            h, route, counts = res
            d_ff = moe_wg.shape[3]
            tf = d_ff // 2 if (d_ff // 2) % LANES == 0 else d_ff
            tile_expert, n_active, zero_tiles, pos_flat, pos_tiles, n_rows = _route_plan(
                route, counts, n_experts, tm_moe, tm_comb)
            xs = _moe_dispatch(pos_flat, zero_tiles, h, n_rows=n_rows, td=tm_comb, tm_rows=tm_moe)
            y_sorted = _moe_experts(tile_expert, n_active, xs,
                                    *expert_w,
                                    tm=tm_moe, tf=tf)
            h = _moe_combine(pos_tiles, y_sorted, route, h, row2(ln2_g[l]), row2(ln2_b[l]),
                             alpha=alpha, tm=tm_comb)
    return h.reshape(batch, seq, d)
```
